```python
import math
import jax
import jax.numpy as jnp
from jax import lax
import numpy as np

D_MODEL = 2048
BATCH = 16
SEQ = 256
DEPTH = 4
DEC_BATCH = 2
DEC_SEQ = 4096
PAST_LEN = 256

GRID_W = 64
BLOCK = 128
EPS = 1e-6
ROPE_BASE = 10000.0

HEAD_DIM = 128
A_HEADS = 4
A_KV_HEADS = 2
A_GROUP = A_HEADS // A_KV_HEADS
A_WINDOW = 128
B_HEADS = 4
B_HALF = HEAD_DIM // 2
C_HEADS = 16
C_HEAD_DIM = 64
C_INNER = C_HEADS * C_HEAD_DIM
C_GROUPS = 2
C_HPG = C_HEADS // C_GROUPS
C_STATE = 128
C_CONV = 5
C_CHUNK = 128
C_CONV_CH = C_INNER + 2 * C_GROUPS * C_STATE

A_Q = A_HEADS * HEAD_DIM
A_KV = A_KV_HEADS * HEAD_DIM
B_QKV = B_HEADS * HEAD_DIM
PROJ_SIZES = (A_Q, A_KV, A_KV, B_QKV, B_QKV, B_QKV, C_INNER, C_CONV_CH, 2 * C_HEADS)
IN_COLS = sum(PROJ_SIZES)
MIX_W = A_Q + B_QKV + C_INNER
FF = (8 * D_MODEL + 3 * 256 - 1) // (3 * 256) * 256
N_MOD = 6

kernel_name = 'hybrid_diffusion_prefix_trunk_step'


def rmsnorm(x, g):
    xf = x.astype(jnp.float32)
    y = xf * lax.rsqrt(jnp.mean(xf * xf, axis=-1, keepdims=True) + EPS)
    return (y * g.astype(jnp.float32)).astype(x.dtype)


def adaln(cond, w, bias):
    m = jax.nn.silu(cond) @ w + bias
    return jnp.split(m[..., None, :], N_MOD, axis=-1)


def split_projection(p):
    cuts, acc = [], 0
    for size in PROJ_SIZES[:-1]:
        acc += size
        cuts.append(acc)
    return jnp.split(p, cuts, axis=-1)


def axial_rope_tables(n, rot_dim):
    rows = n // GRID_W
    row = jnp.repeat(jnp.arange(rows), GRID_W).astype(jnp.float32)
    col = (jnp.arange(rows * GRID_W) % GRID_W).astype(jnp.float32)
    quarter = rot_dim // 4
    inv = ROPE_BASE ** (-jnp.arange(quarter, dtype=jnp.float32) / quarter)
    ang = jnp.concatenate([row[:, None] * inv, col[:, None] * inv], axis=-1)
    return jnp.cos(ang), jnp.sin(ang)


def apply_rope(x, cos, sin):
    half = x.shape[-1] // 2
    shape = (cos.shape[0],) + (1,) * (x.ndim - 3) + (half,)
    cos, sin = cos.reshape(shape), sin.reshape(shape)
    xf = x.astype(jnp.float32)
    x1, x2 = xf[..., :half], xf[..., half:]
    return jnp.concatenate([x1 * cos - x2 * sin, x1 * sin + x2 * cos], axis=-1).astype(x.dtype)


def sweep_query_blocks(fn, q):
    b, n = q.shape[:2]
    nb = n // BLOCK
    qb = jnp.moveaxis(q.reshape((b, nb, BLOCK) + q.shape[2:]), 1, 0)
    out = jnp.moveaxis(lax.map(fn, qb), 0, 1)
    return out.reshape((b, n) + out.shape[3:])


def sink_attention(q, k, v, sink):
    b, m = q.shape[:2]
    qg = q.reshape(b, m, A_KV_HEADS, A_GROUP, HEAD_DIM)
    s = jnp.einsum('bqkgd,bskd->bkgqs', qg, k, preferred_element_type=jnp.float32) * (HEAD_DIM ** -0.5)
    snk = jnp.broadcast_to(sink.astype(jnp.float32).reshape(1, A_KV_HEADS, A_GROUP, 1, 1), s.shape[:-1] + (1,))
    p = jax.nn.softmax(jnp.concatenate([s, snk], axis=-1), axis=-1)[..., :-1]
    o = jnp.einsum('bkgqs,bskd->bqkgd', p.astype(v.dtype), v)
    return o.reshape(b, m, A_HEADS, HEAD_DIM)


def banded_window_attention(q, k, v, k_ctx, v_ctx, sink):
    b, n = q.shape[:2]
    nb = n // BLOCK
    n_ctx = k_ctx.shape[1]
    scale = HEAD_DIM ** -0.5
    qb = q.reshape(b, nb, BLOCK, A_KV_HEADS, A_GROUP, HEAD_DIM)
    pad = ((0, 0), (BLOCK, BLOCK), (0, 0), (0, 0))
    idx = jnp.arange(nb)[:, None] * BLOCK + jnp.arange(3 * BLOCK)[None, :]
    kw = jnp.pad(k, pad)[:, idx]
    vw = jnp.pad(v, pad)[:, idx]
    qpos = idx[:, BLOCK:2 * BLOCK]
    kpos = idx[:, None, :]
    mask = (jnp.abs(qpos[:, :, None] - kpos) <= A_WINDOW) & (kpos >= BLOCK) & (kpos < n + BLOCK)
    s_lat = jnp.einsum('bnqkgd,bnskd->bnkgqs', qb, kw, preferred_element_type=jnp.float32) * scale
    s_lat = jnp.where(mask[None, :, None, None], s_lat, -jnp.inf)
    s_ctx = jnp.einsum('bnqkgd,bckd->bnkgqc', qb, k_ctx, preferred_element_type=jnp.float32) * scale
    snk = jnp.broadcast_to(sink.astype(jnp.float32).reshape(1, 1, A_KV_HEADS, A_GROUP, 1, 1), s_lat.shape[:-1] + (1,))
    p = jax.nn.softmax(jnp.concatenate([s_lat, s_ctx, snk], axis=-1), axis=-1)
    w = 3 * BLOCK
    o = (jnp.einsum('bnkgqs,bnskd->bnqkgd', p[..., :w].astype(v.dtype), vw)
         + jnp.einsum('bnkgqc,bckd->bnqkgd', p[..., w:w + n_ctx].astype(v.dtype), v_ctx))
    return o.reshape(b, n, A_HEADS, HEAD_DIM)


def diff_attention(q, keys, vals, lam):
    s = jnp.concatenate([jnp.einsum('bqhjd,bshjd->bhjqs', q, k, preferred_element_type=jnp.float32)
                         for k in keys], axis=-1) * (B_HALF ** -0.5)
    p = jax.nn.softmax(s, axis=-1)
    w = p[:, :, 0] - lam * p[:, :, 1]
    o, off = 0, 0
    for v in vals:
        sl = v.shape[1]
        o = o + jnp.einsum('bhqs,bshd->bqhd', w[..., off:off + sl].astype(v.dtype), v)
        off += sl
    return o


def depthwise_conv_centred(x, w, bias):
    y = lax.conv_general_dilated(x, w[:, None, :].astype(x.dtype), window_strides=(1,),
                                 padding=[(C_CONV // 2, C_CONV // 2)],
                                 dimension_numbers=('NWC', 'WIO', 'NWC'),
                                 feature_group_count=x.shape[-1])
    return y + bias.astype(x.dtype)


def ssd_chunked_scan(x, dt, a, bm, cm, h0):
    b, t = x.shape[:2]
    nc = t // C_CHUNK
    x = x.reshape(b, nc, C_CHUNK, C_GROUPS, C_HPG, C_HEAD_DIM)
    dt = dt.reshape(b, nc, C_CHUNK, C_GROUPS, C_HPG)
    bm = bm.reshape(b, nc, C_CHUNK, C_GROUPS, C_STATE)
    cm = cm.reshape(b, nc, C_CHUNK, C_GROUPS, C_STATE)
    cum = jnp.cumsum(dt * a.reshape(C_GROUPS, C_HPG), axis=2)
    seg = cum[:, :, :, None] - cum[:, :, None, :]
    tri = jnp.tril(jnp.ones((C_CHUNK, C_CHUNK), bool))[:, :, None, None]
    decay = jnp.exp(jnp.where(tri, seg, -jnp.inf))
    cb = jnp.einsum('bclgn,bcsgn->bclsg', cm, bm)
    wts = cb[..., None] * decay * dt[:, :, None]
    y_diag = jnp.einsum('bclsgh,bcsghp->bclghp', wts, x)
    to_end = jnp.exp(cum[:, :, -1:] - cum) * dt
    states = jnp.einsum('bclgn,bclgh,bclghp->bcghpn', bm, to_end, x)
    chunk_decay = jnp.exp(cum[:, :, -1])

    def step(h, inp):
        s_c, d_c = inp
        return d_c[..., None, None] * h + s_c, h

    h0g = h0.reshape(b, C_GROUPS, C_HPG, C_HEAD_DIM, C_STATE)
    h_last, h_prev = lax.scan(step, h0g, (jnp.moveaxis(states, 1, 0), jnp.moveaxis(chunk_decay, 1, 0)))
    h_prev = jnp.moveaxis(h_prev, 0, 1)
    y_off = jnp.einsum('bclgn,bcghpn->bclghp', cm, h_prev) * jnp.exp(cum)[..., None]
    y = (y_diag + y_off).reshape(b, t, C_HEADS, C_HEAD_DIM)
    return y, h_last.reshape(b, C_HEADS, C_HEAD_DIM, C_STATE)


def ssd_mixer(z, xbc, dt_raw, lw, h0f, h0b):
    b, n = z.shape[:2]
    f32 = jnp.float32
    xbc = jax.nn.silu(depthwise_conv_centred(xbc, lw['conv_w'], lw['conv_b'])).astype(f32)
    x = xbc[..., :C_INNER].reshape(b, n, C_HEADS, C_HEAD_DIM)
    bm = xbc[..., C_INNER:C_INNER + C_GROUPS * C_STATE].reshape(b, n, C_GROUPS, C_STATE)
    cm = xbc[..., C_INNER + C_GROUPS * C_STATE:].reshape(b, n, C_GROUPS, C_STATE)
    dt = jax.nn.softplus(dt_raw.astype(f32).reshape(b, n, 2, C_HEADS) + lw['dt_bias'].astype(f32))
    a = -jnp.exp(lw['a_log'].astype(f32))
    rev = lambda t: jnp.flip(t, axis=1)
    y_f, h_f = ssd_chunked_scan(x, dt[:, :, 0], a[0], bm, cm, h0f.astype(f32))
    y_b, h_b = ssd_chunked_scan(rev(x), rev(dt[:, :, 1]), a[1], rev(bm), rev(cm), h0b.astype(f32))
    y = y_f + rev(y_b) + lw['d_skip'].astype(f32)[:, None] * x
    y = y.reshape(b, n, C_INNER) * jax.nn.silu(z.astype(f32))
    yg = y.reshape(b, n, C_GROUPS, C_INNER // C_GROUPS)
    yg = yg * lax.rsqrt(jnp.mean(yg * yg, axis=-1, keepdims=True) + EPS)
    y = yg.reshape(b, n, C_INNER) * lw['ssm_norm'].astype(f32)
    return y.astype(z.dtype), h_f, h_b


def lambda_init(layer):
    return 0.8 - 0.6 * math.exp(-0.3 * layer)


def trunk_layer(h, cond, lw, lam_init, ctx_cache):
    b, n = h.shape[:2]
    sh1, sc1, g1, sh2, sc2, g2 = adaln(cond, lw['w_ada'], lw['b_ada'])
    u = rmsnorm(h, lw['norm_mix']) * (1 + sc1) + sh1
    qa, ka, va, qd, kd, vd, z, xbc, dt_raw = split_projection(u @ lw['w_in'])
    qa = qa.reshape(b, n, A_HEADS, HEAD_DIM)
    ka = ka.reshape(b, n, A_KV_HEADS, HEAD_DIM)
    va = va.reshape(b, n, A_KV_HEADS, HEAD_DIM)
    qd = qd.reshape(b, n, B_HEADS, 2, B_HALF)
    kd = kd.reshape(b, n, B_HEADS, 2, B_HALF)
    vd = vd.reshape(b, n, B_HEADS, HEAD_DIM)
    lv = lw['diff_lambda'].astype(jnp.float32)
    lam = jnp.exp(jnp.sum(lv[0] * lv[1])) - jnp.exp(jnp.sum(lv[2] * lv[3])) + lam_init
    sink = lw['attn_sink']
    if ctx_cache is None:
        oa = sweep_query_blocks(lambda qi: sink_attention(qi, ka, va, sink), qa)
        od = sweep_query_blocks(lambda qi: diff_attention(qi, (kd,), (vd,), lam), qd)
        h0f = jnp.zeros((b, C_HEADS, C_HEAD_DIM, C_STATE), jnp.float32)
        h0b = h0f
    else:
        ck_a, cv_a, ck_d, cv_d, h0f, h0b = ctx_cache
        cos_a, sin_a = axial_rope_tables(n, HEAD_DIM)
        cos_d, sin_d = axial_rope_tables(n, B_HALF)
        qa_r = apply_rope(qa, cos_a, sin_a)
        ka_r = apply_rope(ka, cos_a, sin_a)
        qd_r = apply_rope(qd, cos_d, sin_d)
        kd_r = apply_rope(kd, cos_d, sin_d)
        oa = banded_window_attention(qa_r, ka_r, va, ck_a, cv_a, sink)
        ck_d = ck_d.reshape(b, ck_d.shape[1], B_HEADS, 2, B_HALF)
        od = sweep_query_blocks(lambda qi: diff_attention(qi, (kd_r, ck_d), (vd, cv_d), lam), qd_r)
    od = rmsnorm(od, lw['diff_norm']) * (1.0 - lam_init)
    oc, hf, hb = ssd_mixer(z, xbc, dt_raw, lw, h0f, h0b)
    mixed = jnp.concatenate([oa.reshape(b, n, A_Q), od.reshape(b, n, B_QKV), oc], axis=-1) @ lw['w_out']
    h = h + g1 * mixed
    u = rmsnorm(h, lw['norm_ffn']) * (1 + sc2) + sh2
    gate, up = jnp.split(u @ lw['w_gate_up'], 2, axis=-1)
    h = h + g2 * ((jax.nn.silu(gate) * up) @ lw['w_down'])
    if ctx_cache is None:
        return h, (ka, va, kd.reshape(b, n, B_HEADS, HEAD_DIM), vd, hf.astype(h.dtype), hb.astype(h.dtype))
    return h, None


def setup_inputs(seed: int = 0) -> dict:
    key = jax.random.key(seed)
    ks = jax.random.split(key, 32)
    f32 = jnp.float32
    nrm = lambda k, shape, scale: jax.random.normal(k, shape, f32) * scale
    dt0 = jnp.exp(jax.random.uniform(ks[20], (DEPTH, 2, C_HEADS), f32) * (math.log(0.1) - math.log(0.001)) + math.log(0.001))
    return {
        'x_prompt': nrm(ks[0], (BATCH, SEQ, D_MODEL), 1.0),
        'x_sample': nrm(ks[1], (DEC_BATCH, DEC_SEQ, D_MODEL), 1.0),
        'cache_attn_k': nrm(ks[2], (DEC_BATCH, DEPTH, PAST_LEN, A_KV_HEADS, HEAD_DIM), 1.0),
        'cache_attn_v': nrm(ks[3], (DEC_BATCH, DEPTH, PAST_LEN, A_KV_HEADS, HEAD_DIM), 1.0),
        'cache_diff_k': nrm(ks[4], (DEC_BATCH, DEPTH, PAST_LEN, B_HEADS, HEAD_DIM), 1.0),
        'cache_diff_v': nrm(ks[5], (DEC_BATCH, DEPTH, PAST_LEN, B_HEADS, HEAD_DIM), 1.0),
        'state_ssm_fwd': nrm(ks[6], (DEC_BATCH, DEPTH, C_HEADS, C_HEAD_DIM, C_STATE), 0.1),
        'state_ssm_bwd': nrm(ks[7], (DEC_BATCH, DEPTH, C_HEADS, C_HEAD_DIM, C_STATE), 0.1),
        'c': nrm(ks[8], (DEC_BATCH, D_MODEL), 1.0),
        'c_ctx': nrm(ks[9], (D_MODEL,), 1.0),
        'w_ada': nrm(ks[10], (DEPTH, D_MODEL, N_MOD * D_MODEL), 0.5 * D_MODEL ** -0.5),
        'b_ada': nrm(ks[11], (DEPTH, N_MOD * D_MODEL), 0.01),
        'norm_mix': 1.0 + nrm(ks[12], (DEPTH, D_MODEL), 0.01),
        'norm_ffn': 1.0 + nrm(ks[13], (DEPTH, D_MODEL), 0.01),
        'w_in': nrm(ks[14], (DEPTH, D_MODEL, IN_COLS), D_MODEL ** -0.5),
        'attn_sink': nrm(ks[15], (DEPTH, A_HEADS), 0.5),
        'diff_lambda': nrm(ks[16], (DEPTH, 4, B_HALF), 0.1),
        'diff_norm': 1.0 + nrm(ks[17], (DEPTH, HEAD_DIM), 0.01),
        'conv_w': nrm(ks[18], (DEPTH, C_CONV, C_CONV_CH), C_CONV ** -0.5),
        'conv_b': nrm(ks[19], (DEPTH, C_CONV_CH), 0.01),
        'dt_bias': dt0 + jnp.log(-jnp.expm1(-dt0)),
        'a_log': jnp.log(jax.random.uniform(ks[21], (DEPTH, 2, C_HEADS), f32, 1.0, 16.0)),
        'd_skip': 1.0 + nrm(ks[22], (DEPTH, C_HEADS), 0.01),
        'ssm_norm': 1.0 + nrm(ks[23], (DEPTH, C_INNER), 0.01),
        'w_out': nrm(ks[24], (DEPTH, MIX_W, D_MODEL), MIX_W ** -0.5),
        'w_gate_up': nrm(ks[25], (DEPTH, D_MODEL, 2 * FF), D_MODEL ** -0.5),
        'w_down': nrm(ks[26], (DEPTH, FF, D_MODEL), FF ** -0.5),
        'norm_final': 1.0 + nrm(ks[27], (D_MODEL,), 0.01),
    }


def reference(x_prompt, x_sample, cache_attn_k, cache_attn_v, cache_diff_k, cache_diff_v,
              state_ssm_fwd, state_ssm_bwd, c, c_ctx, w_ada, b_ada, norm_mix, norm_ffn, w_in,
              attn_sink, diff_lambda, diff_norm, conv_w, conv_b, dt_bias, a_log, d_skip,
              ssm_norm, w_out, w_gate_up, w_down, norm_final):
    def layer_weights(l):
        return {'w_ada': w_ada[l], 'b_ada': b_ada[l], 'norm_mix': norm_mix[l], 'norm_ffn': norm_ffn[l],
                'w_in': w_in[l], 'attn_sink': attn_sink[l], 'diff_lambda': diff_lambda[l],
                'diff_norm': diff_norm[l], 'conv_w': conv_w[l], 'conv_b': conv_b[l],
                'dt_bias': dt_bias[l], 'a_log': a_log[l], 'd_skip': d_skip[l], 'ssm_norm': ssm_norm[l],
                'w_out': w_out[l], 'w_gate_up': w_gate_up[l], 'w_down': w_down[l]}

    h = x_prompt
    ctx_layers = []
    for l in range(DEPTH):
        h, ctx = trunk_layer(h, c_ctx, layer_weights(l), lambda_init(l), None)
        ctx_layers.append(ctx)
    y_prompt = rmsnorm(h, norm_final)
    new_attn_k = jnp.stack([t[0] for t in ctx_layers], axis=1)
    new_attn_v = jnp.stack([t[1] for t in ctx_layers], axis=1)
    new_diff_k = jnp.stack([t[2] for t in ctx_layers], axis=1)
    new_diff_v = jnp.stack([t[3] for t in ctx_layers], axis=1)
    new_ssm_fwd = jnp.stack([t[4] for t in ctx_layers], axis=1)
    new_ssm_bwd = jnp.stack([t[5] for t in ctx_layers], axis=1)

    h = x_sample
    for l in range(DEPTH):
        cache = (cache_attn_k[:, l], cache_attn_v[:, l], cache_diff_k[:, l], cache_diff_v[:, l],
                 state_ssm_fwd[:, l], state_ssm_bwd[:, l])
        h, _ = trunk_layer(h, c, layer_weights(l), lambda_init(l), cache)
    y_sample = rmsnorm(h, norm_final)
    return (y_prompt, y_sample, new_attn_k, new_attn_v, new_diff_k, new_diff_v, new_ssm_fwd, new_ssm_bwd)
```

```python
import functools
import math

import jax
import jax.numpy as jnp
from jax import lax
from jax.experimental import pallas as pl
from jax.experimental.pallas import tpu as pltpu

F32 = jnp.float32
BF16 = jnp.bfloat16

HEAD_DIM = 128
A_HEADS = 4
A_KV_HEADS = 2
A_GROUP = A_HEADS // A_KV_HEADS
BLOCK = 128
B_HEADS = 4
B_HALF = HEAD_DIM // 2
C_HEADS = 16
C_HEAD_DIM = 64
C_INNER = C_HEADS * C_HEAD_DIM
C_GROUPS = 2
C_STATE = 128
C_CONV = 5
C_CHUNK = 128
C_CONV_CH = C_INNER + 2 * C_GROUPS * C_STATE
GRID_W = 64
EPS = 1e-6
ROPE_BASE = 10000.0
N_MOD = 6
LANES = 128
HALO = 8
NEG = -1e30

COL_XBC, COL_QA, COL_Z, COL_KA, COL_VA, COL_QD, COL_KD, COL_VD = 0, 12, 16, 24, 26, 28, 32, 36
P_COLS = 40 * LANES
DT_COLS = 2 * C_HEADS

VMEM_LIMIT = 48 * 1024 * 1024


def _cparams(*sem):
    return pltpu.CompilerParams(dimension_semantics=sem, vmem_limit_bytes=VMEM_LIMIT)


def _dot(a, b):
    return jnp.dot(a, b, preferred_element_type=F32)


def _dot_nt(a, b):
    return lax.dot_general(a, b, (((1,), (1,)), ((), ())), preferred_element_type=F32)


def _sigmoid(x):
    return 1.0 / (1.0 + jnp.exp(-x))


def _split3(a):
    hi = a.astype(BF16)
    r = a - hi.astype(F32)
    mid = r.astype(BF16)
    lo = (r - mid.astype(F32)).astype(BF16)
    return hi, mid, lo


def _ada_kernel(cond_ref, w_ref, b_ref, o_ref):
    s = cond_ref[...]
    s = s * _sigmoid(s)
    o_ref[...] = _dot(s.astype(BF16), w_ref[...].astype(BF16)) + b_ref[...]


def _ada_call(cond8, w_ada, b_ada, tn=1024):
    depth, d, n = w_ada.shape
    return pl.pallas_call(
        _ada_kernel,
        grid=(depth, n // tn),
        in_specs=[
            pl.BlockSpec((8, d), lambda l, j: (0, 0)),
            pl.BlockSpec((None, d, tn), lambda l, j: (l, 0, j)),
            pl.BlockSpec((None, 1, tn), lambda l, j: (l, 0, j)),
        ],
        out_specs=pl.BlockSpec((None, 8, tn), lambda l, j: (l, 0, j)),
        out_shape=jax.ShapeDtypeStruct((depth, 8, n), F32),
        compiler_params=_cparams("parallel", "parallel"),
        name="adaln",
    )(cond8, w_ada, b_ada.reshape(depth, 1, n))


def _modnorm(x, nw, shift, scale):
    ms = jnp.mean(x * x, axis=-1, keepdims=True)
    y = x * lax.rsqrt(ms + EPS) * nw
    return y * (1.0 + scale) + shift


def _inproj_kernel(h_ref, mod_ref, nw_ref, w_ref, wdt_ref, p_ref, dt_ref, u_sc):
    @pl.when(pl.program_id(1) == 0)
    def _():
        u = _modnorm(h_ref[...], nw_ref[...], mod_ref[0:1, :], mod_ref[1:2, :]).astype(BF16)
        u_sc[...] = u
        dt_ref[...] = _dot(u, wdt_ref[...])

    p_ref[...] = _dot(u_sc[...], w_ref[...])


def _inproj_call(h, mod_l, nw, w, wdt, mod_row, tm, tn=1024):
    t, d = h.shape
    return pl.pallas_call(
        _inproj_kernel,
        grid=(t // tm, P_COLS // tn),
        in_specs=[
            pl.BlockSpec((tm, d), lambda i, j: (i, 0)),
            pl.BlockSpec((None, N_MOD, d), lambda i, j: (mod_row(i * tm), 0, 0)),
            pl.BlockSpec((1, d), lambda i, j: (0, 0)),
            pl.BlockSpec((d, tn), lambda i, j: (0, j)),
            pl.BlockSpec((d, LANES), lambda i, j: (0, 0)),
        ],
        out_specs=[
            pl.BlockSpec((tm, tn), lambda i, j: (i, j)),
            pl.BlockSpec((tm, LANES), lambda i, j: (i, 0)),
        ],
        out_shape=[jax.ShapeDtypeStruct((t, P_COLS), F32), jax.ShapeDtypeStruct((t, LANES), F32)],
        scratch_shapes=[pltpu.VMEM((tm, d), BF16)],
        compiler_params=_cparams("parallel", "arbitrary"),
        name="inproj",
    )(h, mod_l, nw, w, wdt)


def _rope_a(x, c, s):
    return x * c + pltpu.roll(x, HEAD_DIM // 2, axis=1) * s


def _rope_d(x, c, s):
    lane = lax.broadcasted_iota(jnp.int32, x.shape, 1)
    q = B_HALF // 2
    partner = jnp.where((lane & (B_HALF - 1)) < q, pltpu.roll(x, LANES - q, axis=1), pltpu.roll(x, q, axis=1))
    return x * c + partner * s


def _sink_softmax_pv(s, sink, v):
    m = jnp.maximum(jnp.max(s, axis=-1, keepdims=True), sink)
    e = jnp.exp(s - m)
    den = jnp.sum(e, axis=-1, keepdims=True) + jnp.exp(sink - m)
    return _dot(e.astype(BF16), v) / den


def _attn_a_ctx_kernel(sink_ref, q_ref, k_ref, v_ref, o_ref, *, layer):
    kv = pl.program_id(1)
    k = k_ref[...].astype(BF16)
    v = v_ref[...].astype(BF16)
    scale = HEAD_DIM ** -0.5
    for g in range(A_GROUP):
        q = (q_ref[:, g * HEAD_DIM:(g + 1) * HEAD_DIM] * scale).astype(BF16)
        s = _dot_nt(q, k)
        o = _sink_softmax_pv(s, sink_ref[layer, kv * A_GROUP + g], v)
        o_ref[:, g * HEAD_DIM:(g + 1) * HEAD_DIM] = o.astype(o_ref.dtype)


def _attn_a_lat_kernel(sink_ref, q_ref, kp_ref, kc_ref, kn_ref, vp_ref, vc_ref, vn_ref,
                       kx_ref, vx_ref, cq_ref, sq_ref, cp_ref, sp_ref, cn_ref, sn_ref,
                       o_ref, *, layer, n_blocks):
    kv = pl.program_id(1)
    nb = pl.program_id(2)
    scale = HEAD_DIM ** -0.5
    cq, sq = cq_ref[...], sq_ref[...]
    k_all = jnp.concatenate([
        _rope_a(kp_ref[...], cp_ref[...], sp_ref[...]).astype(BF16),
        _rope_a(kc_ref[...], cq, sq).astype(BF16),
        _rope_a(kn_ref[...], cn_ref[...], sn_ref[...]).astype(BF16),
        kx_ref[...].astype(BF16)], axis=0)
    v_all = jnp.concatenate([vp_ref[...], vc_ref[...], vn_ref[...], vx_ref[...]], axis=0).astype(BF16)
    n_ctx = kx_ref.shape[0]
    qi = lax.broadcasted_iota(jnp.int32, (BLOCK, 3 * BLOCK + n_ctx), 0)
    kj = lax.broadcasted_iota(jnp.int32, (BLOCK, 3 * BLOCK + n_ctx), 1)
    bad_prev = (kj < BLOCK) & ((kj < qi) | (nb == 0))
    bad_next = (kj >= 2 * BLOCK) & (kj < 3 * BLOCK) & ((kj - 2 * BLOCK > qi) | (nb == n_blocks - 1))
    mask = jnp.logical_not(bad_prev | bad_next)
    for g in range(A_GROUP):
        sl = slice(g * HEAD_DIM, (g + 1) * HEAD_DIM)
        q = (_rope_a(q_ref[:, sl], cq, sq) * scale).astype(BF16)
        s = jnp.where(mask, _dot_nt(q, k_all), NEG)
        o = _sink_softmax_pv(s, sink_ref[layer, kv * A_GROUP + g], v_all)
        o_ref[:, sl] = o.astype(o_ref.dtype)


def _attn_a_calls(p, sink, ck, cv, cos_a, sin_a, layer, dims):
    t = p.shape[0]
    batch, seq, dec_batch, dec_seq = dims
    n_ctx = batch * seq
    smem = pl.BlockSpec(memory_space=pltpu.SMEM)
    qw = A_GROUP * HEAD_DIM
    oa_ctx = pl.pallas_call(
        functools.partial(_attn_a_ctx_kernel, layer=layer),
        grid=(batch, A_KV_HEADS),
        in_specs=[
            smem,
            pl.BlockSpec((seq, qw), lambda b, k: (b, COL_QA // A_GROUP + k)),
            pl.BlockSpec((seq, HEAD_DIM), lambda b, k: (b, COL_KA + k)),
            pl.BlockSpec((seq, HEAD_DIM), lambda b, k: (b, COL_VA + k)),
        ],
        out_specs=pl.BlockSpec((seq, qw), lambda b, k: (b, k)),
        out_shape=jax.ShapeDtypeStruct((n_ctx, A_HEADS * HEAD_DIM), BF16),
        compiler_params=_cparams("parallel", "parallel"),
        name="attn_a_ctx",
    )(sink, p, p, p)

    nbl = dec_seq // BLOCK
    base = n_ctx // BLOCK

    def row(b, nb):
        return base + b * nbl + nb

    def prev(nb):
        return jnp.maximum(nb - 1, 0)

    def nxt(nb):
        return jnp.minimum(nb + 1, nbl - 1)

    past = ck.shape[2]
    blk = (BLOCK, HEAD_DIM)
    oa_lat = pl.pallas_call(
        functools.partial(_attn_a_lat_kernel, layer=layer, n_blocks=nbl),
        grid=(dec_batch, A_KV_HEADS, nbl),
        in_specs=[
            smem,
            pl.BlockSpec((BLOCK, qw), lambda b, k, n: (row(b, n), COL_QA // A_GROUP + k)),
            pl.BlockSpec(blk, lambda b, k, n: (row(b, prev(n)), COL_KA + k)),
            pl.BlockSpec(blk, lambda b, k, n: (row(b, n), COL_KA + k)),
            pl.BlockSpec(blk, lambda b, k, n: (row(b, nxt(n)), COL_KA + k)),
            pl.BlockSpec(blk, lambda b, k, n: (row(b, prev(n)), COL_VA + k)),
            pl.BlockSpec(blk, lambda b, k, n: (row(b, n), COL_VA + k)),
            pl.BlockSpec(blk, lambda b, k, n: (row(b, nxt(n)), COL_VA + k)),
            pl.BlockSpec((None, None, past, HEAD_DIM), lambda b, k, n: (b, layer, 0, k)),
            pl.BlockSpec((None, None, past, HEAD_DIM), lambda b, k, n: (b, layer, 0, k)),
            pl.BlockSpec(blk, lambda b, k, n: (n, 0)),
            pl.BlockSpec(blk, lambda b, k, n: (n, 0)),
            pl.BlockSpec(blk, lambda b, k, n: (prev(n), 0)),
            pl.BlockSpec(blk, lambda b, k, n: (prev(n), 0)),
            pl.BlockSpec(blk, lambda b, k, n: (nxt(n), 0)),
            pl.BlockSpec(blk, lambda b, k, n: (nxt(n), 0)),
        ],
        out_specs=pl.BlockSpec((BLOCK, qw), lambda b, k, n: (b * nbl + n, k)),
        out_shape=jax.ShapeDtypeStruct((dec_batch * dec_seq, A_HEADS * HEAD_DIM), BF16),
        compiler_params=_cparams("parallel", "parallel", "parallel"),
        name="attn_a_lat",
    )(sink, p, p, p, p, p, p, p, ck, cv, cos_a, sin_a, cos_a, sin_a, cos_a, sin_a)
    return oa_ctx, oa_lat


def _diff_lambda(lv, lam_init):
    a = jnp.sum(lv[0:1, :] * lv[1:2, :], axis=-1, keepdims=True)
    b = jnp.sum(lv[2:3, :] * lv[3:4, :], axis=-1, keepdims=True)
    return jnp.exp(a) - jnp.exp(b) + lam_init


def _diff_core(q, k, v, lam, dn_w, lam_init):
    lane = lax.broadcasted_iota(jnp.int32, q.shape, 1)
    qs = q * (B_HALF ** -0.5)
    outs = []
    for half in range(2):
        sel = (lane < B_HALF) if half == 0 else (lane >= B_HALF)
        s = _dot_nt(jnp.where(sel, qs, 0.0).astype(BF16), k)
        e = jnp.exp(s - jnp.max(s, axis=-1, keepdims=True))
        den = jnp.sum(e, axis=-1, keepdims=True)
        outs.append(_dot(e.astype(BF16), v) / den)
    o = outs[0] - lam * outs[1]
    ms = jnp.mean(o * o, axis=-1, keepdims=True)
    return o * lax.rsqrt(ms + EPS) * dn_w * (1.0 - lam_init)


def _attn_b_ctx_kernel(lv_ref, dn_ref, q_ref, k_ref, v_ref, o_ref, *, lam_init):
    lam = _diff_lambda(lv_ref[...], lam_init)
    o = _diff_core(q_ref[...], k_ref[...].astype(BF16), v_ref[...].astype(BF16), lam, dn_ref[...], lam_init)
    o_ref[...] = o.astype(o_ref.dtype)


def _attn_b_lat_kernel(lv_ref, dn_ref, q_ref, k_ref, v_ref, kx_ref, vx_ref, cq_ref, sq_ref,
                       ck_ref, sk_ref, o_ref, k_sc, v_sc, *, lam_init):
    n = k_ref.shape[0]

    @pl.when(pl.program_id(2) == 0)
    def _():
        k_sc[0:n, :] = _rope_d(k_ref[...], ck_ref[...], sk_ref[...]).astype(BF16)
        k_sc[n:, :] = kx_ref[...].astype(BF16)
        v_sc[0:n, :] = v_ref[...].astype(BF16)
        v_sc[n:, :] = vx_ref[...].astype(BF16)

    lam = _diff_lambda(lv_ref[...], lam_init)
    q = _rope_d(q_ref[...], cq_ref[...], sq_ref[...])
    o = _diff_core(q, k_sc[...], v_sc[...], lam, dn_ref[...], lam_init)
    o_ref[...] = o.astype(o_ref.dtype)


def _attn_b_calls(p, lv, dn_w, ck, cv, cos_d, sin_d, layer, lam_init, dims, tq):
    batch, seq, dec_batch, dec_seq = dims
    n_ctx = batch * seq
    hd = HEAD_DIM
    od_ctx = pl.pallas_call(
        functools.partial(_attn_b_ctx_kernel, lam_init=lam_init),
        grid=(batch, B_HEADS),
        in_specs=[
            pl.BlockSpec((4, B_HALF), lambda b, h: (0, 0)),
            pl.BlockSpec((1, hd), lambda b, h: (0, 0)),
            pl.BlockSpec((seq, hd), lambda b, h: (b, COL_QD + h)),
            pl.BlockSpec((seq, hd), lambda b, h: (b, COL_KD + h)),
            pl.BlockSpec((seq, hd), lambda b, h: (b, COL_VD + h)),
        ],
        out_specs=pl.BlockSpec((seq, hd), lambda b, h: (b, h)),
        out_shape=jax.ShapeDtypeStruct((n_ctx, B_HEADS * hd), BF16),
        compiler_params=_cparams("parallel", "parallel"),
        name="attn_b_ctx",
    )(lv, dn_w, p, p, p)

    past = ck.shape[2]
    nq = dec_seq // tq
    qbase = n_ctx // tq
    sbase = n_ctx // dec_seq
    od_lat = pl.pallas_call(
        functools.partial(_attn_b_lat_kernel, lam_init=lam_init),
        grid=(dec_batch, B_HEADS, nq),
        in_specs=[
            pl.BlockSpec((4, B_HALF), lambda b, h, i: (0, 0)),
            pl.BlockSpec((1, hd), lambda b, h, i: (0, 0)),
            pl.BlockSpec((tq, hd), lambda b, h, i: (qbase + b * nq + i, COL_QD + h)),
            pl.BlockSpec((dec_seq, hd), lambda b, h, i: (sbase + b, COL_KD + h)),
            pl.BlockSpec((dec_seq, hd), lambda b, h, i: (sbase + b, COL_VD + h)),
            pl.BlockSpec((None, None, past, hd), lambda b, h, i: (b, layer, 0, h)),
            pl.BlockSpec((None, None, past, hd), lambda b, h, i: (b, layer, 0, h)),
            pl.BlockSpec((tq, hd), lambda b, h, i: (i, 0)),
            pl.BlockSpec((tq, hd), lambda b, h, i: (i, 0)),
            pl.BlockSpec((dec_seq, hd), lambda b, h, i: (0, 0)),
            pl.BlockSpec((dec_seq, hd), lambda b, h, i: (0, 0)),
        ],
        out_specs=pl.BlockSpec((tq, hd), lambda b, h, i: (b * nq + i, h)),
        out_shape=jax.ShapeDtypeStruct((dec_batch * dec_seq, B_HEADS * hd), BF16),
        scratch_shapes=[pltpu.VMEM((dec_seq + past, hd), BF16), pltpu.VMEM((dec_seq + past, hd), BF16)],
        compiler_params=_cparams("parallel", "parallel", "arbitrary"),
        name="attn_b_lat",
    )(lv, dn_w, p, p, p, ck, cv, cos_d, sin_d, cos_d, sin_d)
    return od_ctx, od_lat


def _expand_heads(a, e3):
    return _dot(jnp.concatenate(_split3(a), axis=1), e3)


def _ssd_direction(xact, dt, a_row, e3, h_sc, hoff, fwd, dskip):
    L = C_CHUNK
    li = lax.broadcasted_iota(jnp.int32, (L, L), 0)
    si = lax.broadcasted_iota(jnp.int32, (L, L), 1)
    tri = (li >= si) if fwd else (li <= si)
    tri_b = jnp.where(tri, 1.0, 0.0).astype(BF16)
    dta = dt * a_row
    cum = _dot(jnp.concatenate([tri_b, tri_b, tri_b], axis=1), jnp.concatenate(_split3(dta), axis=0))
    cum_t = cum.T
    dt_t = dt.T
    end = cum[L - 1:L, :] if fwd else cum[0:1, :]
    to_end = jnp.exp(end - cum) * dt
    ecum = jnp.exp(cum)
    x = xact[:, :C_INNER]
    x_te = (x * _expand_heads(to_end, e3)).astype(BF16)
    dec_row = _expand_heads(jnp.broadcast_to(jnp.exp(end), (8, LANES)), e3)[0:1, :]
    lane = lax.broadcasted_iota(jnp.int32, (L, LANES), 1)
    lo = lane < C_HEAD_DIM
    gw = C_INNER // C_GROUPS
    ys = []
    for g in range(C_GROUPS):
        bm = xact[:, C_INNER + g * C_STATE:C_INNER + (g + 1) * C_STATE]
        cm = xact[:, C_INNER + (C_GROUPS + g) * C_STATE:C_INNER + (C_GROUPS + g + 1) * C_STATE]
        cb = _dot_nt(cm.astype(BF16), bm.astype(BF16))
        h_prev = h_sc[:, g * gw:(g + 1) * gw]
        for pair in range(gw // LANES):
            blk = g * (gw // LANES) + pair
            w_parts, c_parts = [], []
            for j in range(2):
                c = hoff + 2 * blk + j
                seg = cum[:, c:c + 1] - cum_t[c:c + 1, :]
                w = cb * jnp.exp(jnp.where(tri, seg, NEG)) * dt_t[c:c + 1, :]
                w_parts.append(w.astype(BF16))
                c_parts.append((cm * ecum[:, c:c + 1]).astype(BF16))
            xb = x[:, blk * LANES:(blk + 1) * LANES]
            hb = h_prev[:, pair * LANES:(pair + 1) * LANES]
            lhs = jnp.concatenate(w_parts + c_parts, axis=1)
            rhs = jnp.concatenate([jnp.where(lo, xb, 0.0), jnp.where(lo, 0.0, xb),
                                   jnp.where(lo, hb, 0.0), jnp.where(lo, 0.0, hb)], axis=0).astype(BF16)
            ys.append(_dot(lhs, rhs))
        st = _dot(bm.T.astype(BF16), x_te[:, g * gw:(g + 1) * gw])
        h_sc[:, g * gw:(g + 1) * gw] = dec_row[:, g * gw:(g + 1) * gw] * h_prev + st
    y = jnp.concatenate(ys, axis=1)
    if dskip is not None:
        y = y + dskip * x
    return y


def _ssd_kernel(xf_ref, xfp_ref, xfn_ref, xb_ref, xbp_ref, xbn_ref, dtf_ref, dtb_ref,
                cw_ref, cbias_ref, dtbias_ref, alog_ref, dskip_ref, e3_ref, h0f_ref, h0b_ref,
                yf_ref, yb_ref, hfo_ref, hbo_ref, xp_sc, hf_sc, hb_sc,
                *, n_ctx_chunks, ctx_cps, lat_cps):
    s = pl.program_id(0)
    is_ctx = s < n_ctx_chunks
    pos = jnp.where(is_ctx, s % ctx_cps, (s - n_ctx_chunks) % lat_cps)
    cps = jnp.where(is_ctx, ctx_cps, lat_cps)
    first = pos == 0
    last = pos == cps - 1

    @pl.when(first & is_ctx)
    def _():
        hf_sc[...] = jnp.zeros_like(hf_sc)
        hb_sc[...] = jnp.zeros_like(hb_sc)

    @pl.when(first & jnp.logical_not(is_ctx))
    def _():
        hf_sc[...] = h0f_ref[...].T
        hb_sc[...] = h0b_ref[...].T

    L = C_CHUNK

    def conv_act(x_ref, xp_ref, xn_ref, has_prev, has_next):
        xp_sc[0:HALO, :] = jnp.where(has_prev, xp_ref[...], 0.0)
        xp_sc[HALO:HALO + L, :] = x_ref[...]
        xp_sc[HALO + L:, :] = jnp.where(has_next, xn_ref[...], 0.0)
        acc = jnp.broadcast_to(cbias_ref[...], (L, C_CONV_CH))
        for k in range(C_CONV):
            acc = acc + xp_sc[pl.ds(HALO - C_CONV // 2 + k, L), :] * cw_ref[k:k + 1, :]
        return acc * _sigmoid(acc)

    def softplus(v):
        return jnp.maximum(v, 0.0) + jnp.log(1.0 + jnp.exp(-jnp.abs(v)))

    a_row = -jnp.exp(alog_ref[...])
    e3 = e3_ref[...]

    xact = conv_act(xf_ref, xfp_ref, xfn_ref, jnp.logical_not(first), jnp.logical_not(last))
    dt = softplus(dtf_ref[...] + dtbias_ref[...])
    yf_ref[...] = _ssd_direction(xact, dt, a_row, e3[0], hf_sc, 0, True, dskip_ref[...])

    xact = conv_act(xb_ref, xbp_ref, xbn_ref, jnp.logical_not(last), jnp.logical_not(first))
    dt = softplus(dtb_ref[...] + dtbias_ref[...])
    yb_ref[...] = _ssd_direction(xact, dt, a_row, e3[1], hb_sc, C_HEADS, False, None)

    @pl.when(last & is_ctx)
    def _():
        hfo_ref[...] = hf_sc[...].T
        hbo_ref[...] = hb_sc[...].T


def _ssd_call(p, dt, conv_w, conv_b, dtbias, alog, dskip_e, e3, h0f, h0b, layer, dims):
    batch, seq, dec_batch, dec_seq = dims
    L = C_CHUNK
    ctx_cps, lat_cps = seq // L, dec_seq // L
    n_ctx_chunks = batch * ctx_cps
    n_chunks = n_ctx_chunks + dec_batch * lat_cps
    t = p.shape[0]
    hpc = L // HALO
    last_halo = t // HALO - 1

    def mirror(s):
        c_ctx = (s // ctx_cps) * ctx_cps + (ctx_cps - 1 - s % ctx_cps)
        r = s - n_ctx_chunks
        c_lat = n_ctx_chunks + (r // lat_cps) * lat_cps + (lat_cps - 1 - r % lat_cps)
        return jnp.where(s < n_ctx_chunks, c_ctx, c_lat)

    def ident(s):
        return s

    def lat_b(s):
        return jnp.maximum(s - n_ctx_chunks, 0) // lat_cps

    def ctx_b(s):
        return jnp.minimum(s // ctx_cps, batch - 1)

    def xspecs(ch):
        return [
            pl.BlockSpec((L, C_CONV_CH), lambda s: (ch(s), COL_XBC)),
            pl.BlockSpec((HALO, C_CONV_CH), lambda s: (jnp.maximum(ch(s) * hpc - 1, 0), COL_XBC)),
            pl.BlockSpec((HALO, C_CONV_CH), lambda s: (jnp.minimum((ch(s) + 1) * hpc, last_halo), COL_XBC)),
        ]

    const2 = lambda s: (0, 0)
    hspec = pl.BlockSpec((None, None, C_INNER, C_STATE), lambda s: (lat_b(s), layer, 0, 0))
    ospec = pl.BlockSpec((None, C_INNER, C_STATE), lambda s: (ctx_b(s), 0, 0))
    return pl.pallas_call(
        functools.partial(_ssd_kernel, n_ctx_chunks=n_ctx_chunks, ctx_cps=ctx_cps, lat_cps=lat_cps),
        grid=(n_chunks,),
        in_specs=xspecs(ident) + xspecs(mirror) + [
            pl.BlockSpec((L, LANES), lambda s: (s, 0)),
            pl.BlockSpec((L, LANES), lambda s: (mirror(s), 0)),
            pl.BlockSpec((C_CONV, C_CONV_CH), const2),
            pl.BlockSpec((1, C_CONV_CH), const2),
            pl.BlockSpec((1, LANES), const2),
            pl.BlockSpec((1, LANES), const2),
            pl.BlockSpec((1, C_INNER), const2),
            pl.BlockSpec((2, 3 * LANES, C_INNER), lambda s: (0, 0, 0)),
            hspec, hspec,
        ],
        out_specs=[
            pl.BlockSpec((L, C_INNER), lambda s: (s, 0)),
            pl.BlockSpec((L, C_INNER), lambda s: (mirror(s), 0)),
            ospec, ospec,
        ],
        out_shape=[
            jax.ShapeDtypeStruct((t, C_INNER), F32), jax.ShapeDtypeStruct((t, C_INNER), F32),
            jax.ShapeDtypeStruct((batch, C_INNER, C_STATE), F32),
            jax.ShapeDtypeStruct((batch, C_INNER, C_STATE), F32),
        ],
        scratch_shapes=[
            pltpu.VMEM((L + 2 * HALO, C_CONV_CH), F32),
            pltpu.VMEM((C_STATE, C_INNER), F32),
            pltpu.VMEM((C_STATE, C_INNER), F32),
        ],
        compiler_params=_cparams("arbitrary"),
        name="ssd",
    )(p, p, p, p, p, p, dt, dt, conv_w, conv_b, dtbias, alog, dskip_e, e3, h0f, h0b)


def _outproj_kernel(oa_ref, od_ref, yf_ref, yb_ref, z_ref, sn_ref, h_ref, mod_ref, w_ref, o_ref, mix_sc):
    j = pl.program_id(1)
    tn = o_ref.shape[1]

    @pl.when(j == 0)
    def _():
        na = oa_ref.shape[1]
        nd = od_ref.shape[1]
        mix_sc[:, 0:na] = oa_ref[...]
        mix_sc[:, na:na + nd] = od_ref[...]
        z = z_ref[...]
        y = (yf_ref[...] + yb_ref[...]) * (z * _sigmoid(z))
        gw = C_INNER // C_GROUPS
        for g in range(C_GROUPS):
            yg = y[:, g * gw:(g + 1) * gw]
            yg = yg * lax.rsqrt(jnp.mean(yg * yg, axis=-1, keepdims=True) + EPS)
            yg = yg * sn_ref[:, g * gw:(g + 1) * gw]
            mix_sc[:, na + nd + g * gw:na + nd + (g + 1) * gw] = yg.astype(BF16)

    gate = mod_ref[2:3, pl.ds(pl.multiple_of(j * tn, tn), tn)]
    o_ref[...] = h_ref[...] + gate * _dot(mix_sc[...], w_ref[...])


def _outproj_call(oa, od, yf, yb, p, ssm_norm, h, mod_l, w, mod_row, tm, tn=1024):
    t, d = h.shape
    tn = min(tn, d)
    mw = w.shape[0]
    zblk = C_INNER // LANES
    return pl.pallas_call(
        _outproj_kernel,
        grid=(t // tm, d // tn),
        in_specs=[
            pl.BlockSpec((tm, oa.shape[1]), lambda i, j: (i, 0)),
            pl.BlockSpec((tm, od.shape[1]), lambda i, j: (i, 0)),
            pl.BlockSpec((tm, C_INNER), lambda i, j: (i, 0)),
            pl.BlockSpec((tm, C_INNER), lambda i, j: (i, 0)),
            pl.BlockSpec((tm, C_INNER), lambda i, j: (i, COL_Z // zblk)),
            pl.BlockSpec((1, C_INNER), lambda i, j: (0, 0)),
            pl.BlockSpec((tm, tn), lambda i, j: (i, j)),
            pl.BlockSpec((None, N_MOD, d), lambda i, j: (mod_row(i * tm), 0, 0)),
            pl.BlockSpec((mw, tn), lambda i, j: (0, j)),
        ],
        out_specs=pl.BlockSpec((tm, tn), lambda i, j: (i, j)),
        out_shape=jax.ShapeDtypeStruct((t, d), F32),
        scratch_shapes=[pltpu.VMEM((tm, mw), BF16)],
        compiler_params=_cparams("parallel", "arbitrary"),
        name="outproj",
    )(oa, od, yf, yb, p, ssm_norm, h, mod_l, w)


def _ffn_kernel(h_ref, mod_ref, nw_ref, wg_ref, wu_ref, wd_ref, o_ref, u_sc, acc_sc):
    f = pl.program_id(1)

    @pl.when(f == 0)
    def _():
        u_sc[...] = _modnorm(h_ref[...], nw_ref[...], mod_ref[3:4, :], mod_ref[4:5, :]).astype(BF16)
        acc_sc[...] = jnp.zeros_like(acc_sc)

    u = u_sc[...]
    g = _dot(u, wg_ref[...])
    a = (g * _sigmoid(g)) * _dot(u, wu_ref[...])
    acc_sc[...] += _dot(a.astype(BF16), wd_ref[...])

    @pl.when(f == pl.num_programs(1) - 1)
    def _():
        o_ref[...] = h_ref[...] + mod_ref[5:6, :] * acc_sc[...]


def _ffn_call(h, mod_l, nw, w_gu, w_d, mod_row, tm, tf=512):
    t, d = h.shape
    ff = w_d.shape[0]
    nf = ff // tf
    return pl.pallas_call(
        _ffn_kernel,
        grid=(t // tm, nf),
        in_specs=[
            pl.BlockSpec((tm, d), lambda i, f: (i, 0)),
            pl.BlockSpec((None, N_MOD, d), lambda i, f: (mod_row(i * tm), 0, 0)),
            pl.BlockSpec((1, d), lambda i, f: (0, 0)),
            pl.BlockSpec((d, tf), lambda i, f: (0, f)),
            pl.BlockSpec((d, tf), lambda i, f: (0, nf + f)),
            pl.BlockSpec((tf, d), lambda i, f: (f, 0)),
        ],
        out_specs=pl.BlockSpec((tm, d), lambda i, f: (i, 0)),
        out_shape=jax.ShapeDtypeStruct((t, d), F32),
        scratch_shapes=[pltpu.VMEM((tm, d), BF16), pltpu.VMEM((tm, d), F32)],
        compiler_params=_cparams("parallel", "arbitrary"),
        name="ffn",
    )(h, mod_l, nw, w_gu, w_gu, w_d)


def _final_norm_kernel(h_ref, w_ref, o_ref):
    x = h_ref[...]
    o_ref[...] = x * lax.rsqrt(jnp.mean(x * x, axis=-1, keepdims=True) + EPS) * w_ref[...]


def _final_norm_call(h, w, tm):
    t, d = h.shape
    return pl.pallas_call(
        _final_norm_kernel,
        grid=(t // tm,),
        in_specs=[pl.BlockSpec((tm, d), lambda i: (i, 0)), pl.BlockSpec((1, d), lambda i: (0, 0))],
        out_specs=pl.BlockSpec((tm, d), lambda i: (i, 0)),
        out_shape=jax.ShapeDtypeStruct((t, d), F32),
        compiler_params=_cparams("parallel"),
        name="final_norm",
    )(h, w)


def _rope_tables(n, rot_dim):
    rows = n // GRID_W
    row = jnp.repeat(jnp.arange(rows), GRID_W).astype(F32)
    col = (jnp.arange(rows * GRID_W) % GRID_W).astype(F32)
    quarter = rot_dim // 4
    inv = ROPE_BASE ** (-jnp.arange(quarter, dtype=F32) / quarter)
    ang = jnp.concatenate([row[:, None] * inv, col[:, None] * inv], axis=-1)
    c, s = jnp.cos(ang), jnp.sin(ang)
    reps = LANES // rot_dim
    return (jnp.tile(jnp.concatenate([c, c], axis=-1), (1, reps)),
            jnp.tile(jnp.concatenate([-s, s], axis=-1), (1, reps)))


def _lambda_init(layer):
    return 0.8 - 0.6 * math.exp(-0.3 * layer)


def _permute_w_in(w):
    sizes = (A_HEADS * HEAD_DIM, A_KV_HEADS * HEAD_DIM, A_KV_HEADS * HEAD_DIM,
             B_HEADS * HEAD_DIM, B_HEADS * HEAD_DIM, B_HEADS * HEAD_DIM, C_INNER, C_CONV_CH, DT_COLS)
    offs = [0]
    for sz in sizes:
        offs.append(offs[-1] + sz)
    qa, ka, va, qd, kd, vd, z, xbc, dt = [w[..., offs[i]:offs[i + 1]] for i in range(len(sizes))]
    main = jnp.concatenate([xbc, qa, z, ka, va, qd, kd, vd], axis=-1).astype(BF16)
    dtw = jnp.pad(dt, ((0, 0), (0, 0), (0, LANES - DT_COLS))).astype(BF16)
    return main, dtw


def _pad_lanes(v):
    return jnp.pad(v.reshape(v.shape[0], 1, -1), ((0, 0), (0, 0), (0, LANES - DT_COLS)))


def kernel(x_prompt, x_sample, cache_attn_k, cache_attn_v, cache_diff_k, cache_diff_v, state_ssm_fwd, state_ssm_bwd, c, c_ctx, w_ada, b_ada, norm_mix, norm_ffn, w_in, attn_sink, diff_lambda, diff_norm, conv_w, conv_b, dt_bias, a_log, d_skip, ssm_norm, w_out, w_gate_up, w_down, norm_final):
    batch, seq, d = x_prompt.shape
    dec_batch, dec_seq, _ = x_sample.shape
    depth = w_in.shape[0]
    past = cache_attn_k.shape[2]
    n_ctx = batch * seq
    dims = (batch, seq, dec_batch, dec_seq)
    tm = 512 if (n_ctx % 512 == 0 and dec_seq % 512 == 0) else 256
    tq = min(256, dec_seq)
    assert n_ctx % dec_seq == 0 and n_ctx % tm == 0 and dec_seq % tm == 0
    assert seq % C_CHUNK == 0 and dec_seq % C_CHUNK == 0 and dec_seq % GRID_W == 0
    assert 1 + dec_batch <= 8

    def mod_row(start):
        return jnp.where(start < n_ctx, 0, 1 + (start - n_ctx) // dec_seq)

    w_in_main, w_in_dt = _permute_w_in(w_in)
    w_out_b = w_out.astype(BF16)
    w_gu_b = w_gate_up.astype(BF16)
    w_down_b = w_down.astype(BF16)
    cos_a, sin_a = _rope_tables(dec_seq, HEAD_DIM)
    cos_d, sin_d = _rope_tables(dec_seq, B_HALF)
    dtbias = _pad_lanes(dt_bias)
    alog = _pad_lanes(a_log)
    dskip_e = jnp.repeat(d_skip, C_HEAD_DIM, axis=-1).reshape(depth, 1, C_INNER)
    head_of_lane = jnp.arange(C_INNER) // C_HEAD_DIM
    e_f = (jnp.arange(LANES)[:, None] == head_of_lane[None, :]).astype(BF16)
    e_b = (jnp.arange(LANES)[:, None] == head_of_lane[None, :] + C_HEADS).astype(BF16)
    e3 = jnp.stack([jnp.concatenate([e_f] * 3, axis=0), jnp.concatenate([e_b] * 3, axis=0)])
    ck_a = cache_attn_k.reshape(dec_batch, depth, past, A_KV_HEADS * HEAD_DIM)
    cv_a = cache_attn_v.reshape(dec_batch, depth, past, A_KV_HEADS * HEAD_DIM)
    ck_d = cache_diff_k.reshape(dec_batch, depth, past, B_HEADS * HEAD_DIM)
    cv_d = cache_diff_v.reshape(dec_batch, depth, past, B_HEADS * HEAD_DIM)
    h0f = state_ssm_fwd.reshape(dec_batch, depth, C_INNER, C_STATE)
    h0b = state_ssm_bwd.reshape(dec_batch, depth, C_INNER, C_STATE)

    cond8 = jnp.concatenate([c_ctx[None, :], c, jnp.zeros((8 - 1 - dec_batch, d), F32)], axis=0)
    mod = _ada_call(cond8, w_ada, b_ada)[:, :1 + dec_batch].reshape(depth, 1 + dec_batch, N_MOD, d)

    h = jnp.concatenate([x_prompt.reshape(n_ctx, d), x_sample.reshape(dec_batch * dec_seq, d)], axis=0)
    ctx_out = []
    for l in range(depth):
        lam_init = _lambda_init(l)
        p, dt = _inproj_call(h, mod[l], norm_mix[l][None, :], w_in_main[l], w_in_dt[l], mod_row, tm)
        oa_c, oa_l = _attn_a_calls(p, attn_sink, ck_a, cv_a, cos_a, sin_a, l, dims)
        od_c, od_l = _attn_b_calls(p, diff_lambda[l], diff_norm[l][None, :], ck_d, cv_d, cos_d, sin_d,
                                   l, lam_init, dims, tq)
        yf, yb, hf, hb = _ssd_call(p, dt, conv_w[l], conv_b[l][None, :], dtbias[l], alog[l], dskip_e[l],
                                   e3, h0f, h0b, l, dims)
        oa = jnp.concatenate([oa_c, oa_l], axis=0)
        od = jnp.concatenate([od_c, od_l], axis=0)
        h = _outproj_call(oa, od, yf, yb, p, ssm_norm[l][None, :], h, mod[l], w_out_b[l], mod_row, tm)
        h = _ffn_call(h, mod[l], norm_ffn[l][None, :], w_gu_b[l], w_down_b[l], mod_row, tm)
        pc = p[:n_ctx].reshape(batch, seq, P_COLS)
        ka = pc[..., COL_KA * LANES:COL_VA * LANES].reshape(batch, seq, A_KV_HEADS, HEAD_DIM)
        va = pc[..., COL_VA * LANES:COL_QD * LANES].reshape(batch, seq, A_KV_HEADS, HEAD_DIM)
        kd = pc[..., COL_KD * LANES:COL_VD * LANES].reshape(batch, seq, B_HEADS, HEAD_DIM)
        vd = pc[..., COL_VD * LANES:].reshape(batch, seq, B_HEADS, HEAD_DIM)
        ctx_out.append((ka, va, kd, vd, hf.reshape(batch, C_HEADS, C_HEAD_DIM, C_STATE),
                        hb.reshape(batch, C_HEADS, C_HEAD_DIM, C_STATE)))

    y = _final_norm_call(h, norm_final[None, :], tm)
    y_prompt = y[:n_ctx].reshape(batch, seq, d)
    y_sample = y[n_ctx:].reshape(dec_batch, dec_seq, d)
    stacked = [jnp.stack([t[i] for t in ctx_out], axis=1) for i in range(6)]
    return (y_prompt, y_sample, *stacked)
```

```python
import functools
import math

import jax
import jax.numpy as jnp
from jax import lax
from jax.experimental import pallas as pl
from jax.experimental.pallas import tpu as pltpu

F32 = jnp.float32
BF16 = jnp.bfloat16

HEAD_DIM = 128
A_HEADS = 4
A_KV_HEADS = 2
A_GROUP = A_HEADS // A_KV_HEADS
BLOCK = 128
B_HEADS = 4
B_HALF = HEAD_DIM // 2
C_HEADS = 16
C_HEAD_DIM = 64
C_INNER = C_HEADS * C_HEAD_DIM
C_GROUPS = 2
C_STATE = 128
C_CONV = 5
C_CHUNK = 128
C_CONV_CH = C_INNER + 2 * C_GROUPS * C_STATE
GRID_W = 64
EPS = 1e-6
ROPE_BASE = 10000.0
N_MOD = 6
LANES = 128
HALO = 8
NEG = -1e30
LOG2E = math.log2(math.e)

COL_XBC, COL_QA, COL_Z, COL_KA, COL_VA, COL_QD, COL_KD, COL_VD = 0, 12, 16, 24, 26, 28, 32, 36
P_COLS = 40 * LANES
DT_COLS = 2 * C_HEADS

VMEM_LIMIT = 48 * 1024 * 1024


def _cparams(*sem):
    return pltpu.CompilerParams(dimension_semantics=sem, vmem_limit_bytes=VMEM_LIMIT)


def _dot(a, b):
    return jnp.dot(a, b, preferred_element_type=F32)


def _dot_nt(a, b):
    return lax.dot_general(a, b, (((1,), (1,)), ((), ())), preferred_element_type=F32)


def _sigmoid(x):
    return 1.0 / (1.0 + jnp.exp(-x))


def _split3(a):
    hi = a.astype(BF16)
    r = a - hi.astype(F32)
    mid = r.astype(BF16)
    lo = (r - mid.astype(F32)).astype(BF16)
    return hi, mid, lo


def _ada_kernel(cond_ref, w_ref, b_ref, o_ref):
    s = cond_ref[...]
    s = s * _sigmoid(s)
    o_ref[...] = _dot(s.astype(BF16), w_ref[...].astype(BF16)) + b_ref[...]


def _ada_call(cond8, w_ada, b_ada, tn=1024):
    depth, d, n = w_ada.shape
    return pl.pallas_call(
        _ada_kernel,
        grid=(depth, n // tn),
        in_specs=[
            pl.BlockSpec((8, d), lambda l, j: (0, 0)),
            pl.BlockSpec((None, d, tn), lambda l, j: (l, 0, j)),
            pl.BlockSpec((None, 1, tn), lambda l, j: (l, 0, j)),
        ],
        out_specs=pl.BlockSpec((None, 8, tn), lambda l, j: (l, 0, j)),
        out_shape=jax.ShapeDtypeStruct((depth, 8, n), F32),
        compiler_params=_cparams("parallel", "parallel"),
        name="adaln",
    )(cond8, w_ada, b_ada.reshape(depth, 1, n))


def _modnorm(x, nw, shift, scale):
    ms = jnp.mean(x * x, axis=-1, keepdims=True)
    y = x * lax.rsqrt(ms + EPS) * nw
    return y * (1.0 + scale) + shift


def _inproj_kernel(h_ref, mod_ref, nw_ref, w_ref, wdt_ref, p_ref, dt_ref, u_sc):
    @pl.when(pl.program_id(1) == 0)
    def _():
        u = _modnorm(h_ref[...], nw_ref[...], mod_ref[0:1, :], mod_ref[1:2, :]).astype(BF16)
        u_sc[...] = u
        dt_ref[...] = _dot(u, wdt_ref[...])

    p_ref[...] = _dot(u_sc[...], w_ref[...])


def _inproj_call(h, mod_l, nw, w, wdt, mod_row, tm, tn=1024):
    t, d = h.shape
    return pl.pallas_call(
        _inproj_kernel,
        grid=(t // tm, P_COLS // tn),
        in_specs=[
            pl.BlockSpec((tm, d), lambda i, j: (i, 0)),
            pl.BlockSpec((None, N_MOD, d), lambda i, j: (mod_row(i * tm), 0, 0)),
            pl.BlockSpec((1, d), lambda i, j: (0, 0)),
            pl.BlockSpec((d, tn), lambda i, j: (0, j)),
            pl.BlockSpec((d, LANES), lambda i, j: (0, 0)),
        ],
        out_specs=[
            pl.BlockSpec((tm, tn), lambda i, j: (i, j)),
            pl.BlockSpec((tm, LANES), lambda i, j: (i, 0)),
        ],
        out_shape=[jax.ShapeDtypeStruct((t, P_COLS), F32), jax.ShapeDtypeStruct((t, LANES), F32)],
        scratch_shapes=[pltpu.VMEM((tm, d), BF16)],
        compiler_params=_cparams("parallel", "arbitrary"),
        name="inproj",
    )(h, mod_l, nw, w, wdt)


def _rope_a(x, c, s):
    return x * c + pltpu.roll(x, HEAD_DIM // 2, axis=1) * s


def _rope_d(x, c, s):
    lane = lax.broadcasted_iota(jnp.int32, x.shape, 1)
    q = B_HALF // 2
    partner = jnp.where((lane & (B_HALF - 1)) < q, pltpu.roll(x, LANES - q, axis=1), pltpu.roll(x, q, axis=1))
    return x * c + partner * s


def _sink_softmax_pv(s, sink, v):
    m = jnp.maximum(jnp.max(s, axis=-1, keepdims=True), sink)
    e = jnp.exp2(s - m)
    den = jnp.sum(e, axis=-1, keepdims=True) + jnp.exp2(sink - m)
    return _dot(e.astype(BF16), v) / den


def _attn_a_ctx_kernel(sink_ref, q_ref, k_ref, v_ref, o_ref, *, layer):
    kv = pl.program_id(1)
    k = k_ref[...].astype(BF16)
    v = v_ref[...].astype(BF16)
    scale = HEAD_DIM ** -0.5 * LOG2E
    for g in range(A_GROUP):
        q = (q_ref[:, g * HEAD_DIM:(g + 1) * HEAD_DIM] * scale).astype(BF16)
        s = _dot_nt(q, k)
        o = _sink_softmax_pv(s, sink_ref[layer, kv * A_GROUP + g] * LOG2E, v)
        o_ref[:, g * HEAD_DIM:(g + 1) * HEAD_DIM] = o.astype(o_ref.dtype)


def _attn_a_lat_kernel(sink_ref, q_ref, kp_ref, kc_ref, kn_ref, vp_ref, vc_ref, vn_ref,
                       kx_ref, vx_ref, cq_ref, sq_ref, cp_ref, sp_ref, cn_ref, sn_ref,
                       ctx_rows_ref, o_ref, *, layer, n_blocks):
    del ctx_rows_ref
    kv = pl.program_id(1)
    nb = pl.program_id(2)
    scale = HEAD_DIM ** -0.5 * LOG2E
    cq, sq = cq_ref[...], sq_ref[...]
    k_all = jnp.concatenate([
        _rope_a(kp_ref[...], cp_ref[...], sp_ref[...]).astype(BF16),
        _rope_a(kc_ref[...], cq, sq).astype(BF16),
        _rope_a(kn_ref[...], cn_ref[...], sn_ref[...]).astype(BF16),
        kx_ref[...].astype(BF16)], axis=0)
    v_all = jnp.concatenate([vp_ref[...], vc_ref[...], vn_ref[...], vx_ref[...]], axis=0).astype(BF16)
    n_ctx = kx_ref.shape[0]
    qi = lax.broadcasted_iota(jnp.int32, (BLOCK, 3 * BLOCK + n_ctx), 0)
    kj = lax.broadcasted_iota(jnp.int32, (BLOCK, 3 * BLOCK + n_ctx), 1)
    bad_prev = (kj < BLOCK) & ((kj < qi) | (nb == 0))
    bad_next = (kj >= 2 * BLOCK) & (kj < 3 * BLOCK) & ((kj - 2 * BLOCK > qi) | (nb == n_blocks - 1))
    mask = jnp.logical_not(bad_prev | bad_next)
    for g in range(A_GROUP):
        sl = slice(g * HEAD_DIM, (g + 1) * HEAD_DIM)
        q = (_rope_a(q_ref[:, sl], cq, sq) * scale).astype(BF16)
        s = jnp.where(mask, _dot_nt(q, k_all), NEG)
        o = _sink_softmax_pv(s, sink_ref[layer, kv * A_GROUP + g] * LOG2E, v_all)
        o_ref[:, sl] = o.astype(o_ref.dtype)


def _attn_a_calls(p, sink, ck, cv, cos_a, sin_a, layer, dims):
    t = p.shape[0]
    batch, seq, dec_batch, dec_seq = dims
    n_ctx = batch * seq
    smem = pl.BlockSpec(memory_space=pltpu.SMEM)
    qw = A_GROUP * HEAD_DIM
    oa_ctx = pl.pallas_call(
        functools.partial(_attn_a_ctx_kernel, layer=layer),
        grid=(batch, A_KV_HEADS),
        in_specs=[
            smem,
            pl.BlockSpec((seq, qw), lambda b, k: (b, COL_QA // A_GROUP + k)),
            pl.BlockSpec((seq, HEAD_DIM), lambda b, k: (b, COL_KA + k)),
            pl.BlockSpec((seq, HEAD_DIM), lambda b, k: (b, COL_VA + k)),
        ],
        out_specs=pl.BlockSpec((seq, qw), lambda b, k: (b, k)),
        out_shape=jax.ShapeDtypeStruct((t, A_HEADS * HEAD_DIM), BF16),
        compiler_params=_cparams("parallel", "parallel"),
        name="attn_a_ctx",
    )(sink, p, p, p)

    nbl = dec_seq // BLOCK
    base = n_ctx // BLOCK

    def row(b, nb):
        return base + b * nbl + nb

    def prev(nb):
        return jnp.maximum(nb - 1, 0)

    def nxt(nb):
        return jnp.minimum(nb + 1, nbl - 1)

    past = ck.shape[2]
    blk = (BLOCK, HEAD_DIM)
    oa_lat = pl.pallas_call(
        functools.partial(_attn_a_lat_kernel, layer=layer, n_blocks=nbl),
        grid=(dec_batch, A_KV_HEADS, nbl),
        in_specs=[
            smem,
            pl.BlockSpec((BLOCK, qw), lambda b, k, n: (row(b, n), COL_QA // A_GROUP + k)),
            pl.BlockSpec(blk, lambda b, k, n: (row(b, prev(n)), COL_KA + k)),
            pl.BlockSpec(blk, lambda b, k, n: (row(b, n), COL_KA + k)),
            pl.BlockSpec(blk, lambda b, k, n: (row(b, nxt(n)), COL_KA + k)),
            pl.BlockSpec(blk, lambda b, k, n: (row(b, prev(n)), COL_VA + k)),
            pl.BlockSpec(blk, lambda b, k, n: (row(b, n), COL_VA + k)),
            pl.BlockSpec(blk, lambda b, k, n: (row(b, nxt(n)), COL_VA + k)),
            pl.BlockSpec((None, None, past, HEAD_DIM), lambda b, k, n: (b, layer, 0, k)),
            pl.BlockSpec((None, None, past, HEAD_DIM), lambda b, k, n: (b, layer, 0, k)),
            pl.BlockSpec(blk, lambda b, k, n: (n, 0)),
            pl.BlockSpec(blk, lambda b, k, n: (n, 0)),
            pl.BlockSpec(blk, lambda b, k, n: (prev(n), 0)),
            pl.BlockSpec(blk, lambda b, k, n: (prev(n), 0)),
            pl.BlockSpec(blk, lambda b, k, n: (nxt(n), 0)),
            pl.BlockSpec(blk, lambda b, k, n: (nxt(n), 0)),
            pl.BlockSpec(memory_space=pl.ANY),
        ],
        out_specs=pl.BlockSpec((BLOCK, qw), lambda b, k, n: (row(b, n), k)),
        out_shape=jax.ShapeDtypeStruct((t, A_HEADS * HEAD_DIM), BF16),
        input_output_aliases={16: 0},
        compiler_params=_cparams("parallel", "parallel", "parallel"),
        name="attn_a_lat",
    )(sink, p, p, p, p, p, p, p, ck, cv, cos_a, sin_a, cos_a, sin_a, cos_a, sin_a, oa_ctx)
    return oa_lat


def _diff_lambda(lv, lam_init):
    a = jnp.sum(lv[0:1, :] * lv[1:2, :], axis=-1, keepdims=True)
    b = jnp.sum(lv[2:3, :] * lv[3:4, :], axis=-1, keepdims=True)
    return jnp.exp(a) - jnp.exp(b) + lam_init


def _diff_core(q, k, v, lam, dn_w, lam_init):
    lane = lax.broadcasted_iota(jnp.int32, q.shape, 1)
    qs = q * (B_HALF ** -0.5 * LOG2E)
    outs = []
    for half in range(2):
        sel = (lane < B_HALF) if half == 0 else (lane >= B_HALF)
        s = _dot_nt(jnp.where(sel, qs, 0.0).astype(BF16), k)
        e = jnp.exp2(s - jnp.max(s, axis=-1, keepdims=True))
        den = jnp.sum(e, axis=-1, keepdims=True)
        outs.append(_dot(e.astype(BF16), v) / den)
    o = outs[0] - lam * outs[1]
    ms = jnp.mean(o * o, axis=-1, keepdims=True)
    return o * lax.rsqrt(ms + EPS) * dn_w * (1.0 - lam_init)


def _attn_b_ctx_kernel(lv_ref, dn_ref, q_ref, k_ref, v_ref, o_ref, *, lam_init):
    lam = _diff_lambda(lv_ref[...], lam_init)
    o = _diff_core(q_ref[...], k_ref[...].astype(BF16), v_ref[...].astype(BF16), lam, dn_ref[...], lam_init)
    o_ref[...] = o.astype(o_ref.dtype)


def _attn_b_lat_kernel(lv_ref, dn_ref, q_ref, k_ref, v_ref, kx_ref, vx_ref, cq_ref, sq_ref,
                       ck_ref, sk_ref, ctx_rows_ref, o_ref, k_sc, v_sc, *, lam_init):
    del ctx_rows_ref
    n = k_ref.shape[0]

    @pl.when(pl.program_id(2) == 0)
    def _():
        k_sc[0:n, :] = _rope_d(k_ref[...], ck_ref[...], sk_ref[...]).astype(BF16)
        k_sc[n:, :] = kx_ref[...].astype(BF16)
        v_sc[0:n, :] = v_ref[...].astype(BF16)
        v_sc[n:, :] = vx_ref[...].astype(BF16)

    lam = _diff_lambda(lv_ref[...], lam_init)
    q = _rope_d(q_ref[...], cq_ref[...], sq_ref[...])
    o = _diff_core(q, k_sc[...], v_sc[...], lam, dn_ref[...], lam_init)
    o_ref[...] = o.astype(o_ref.dtype)


def _attn_b_calls(p, lv, dn_w, ck, cv, cos_d, sin_d, layer, lam_init, dims, tq):
    batch, seq, dec_batch, dec_seq = dims
    n_ctx = batch * seq
    hd = HEAD_DIM
    od_ctx = pl.pallas_call(
        functools.partial(_attn_b_ctx_kernel, lam_init=lam_init),
        grid=(batch, B_HEADS),
        in_specs=[
            pl.BlockSpec((4, B_HALF), lambda b, h: (0, 0)),
            pl.BlockSpec((1, hd), lambda b, h: (0, 0)),
            pl.BlockSpec((seq, hd), lambda b, h: (b, COL_QD + h)),
            pl.BlockSpec((seq, hd), lambda b, h: (b, COL_KD + h)),
            pl.BlockSpec((seq, hd), lambda b, h: (b, COL_VD + h)),
        ],
        out_specs=pl.BlockSpec((seq, hd), lambda b, h: (b, h)),
        out_shape=jax.ShapeDtypeStruct((p.shape[0], B_HEADS * hd), BF16),
        compiler_params=_cparams("parallel", "parallel"),
        name="attn_b_ctx",
    )(lv, dn_w, p, p, p)

    past = ck.shape[2]
    nq = dec_seq // tq
    qbase = n_ctx // tq
    sbase = n_ctx // dec_seq
    od_lat = pl.pallas_call(
        functools.partial(_attn_b_lat_kernel, lam_init=lam_init),
        grid=(dec_batch, B_HEADS, nq),
        in_specs=[
            pl.BlockSpec((4, B_HALF), lambda b, h, i: (0, 0)),
            pl.BlockSpec((1, hd), lambda b, h, i: (0, 0)),
            pl.BlockSpec((tq, hd), lambda b, h, i: (qbase + b * nq + i, COL_QD + h)),
            pl.BlockSpec((dec_seq, hd), lambda b, h, i: (sbase + b, COL_KD + h)),
            pl.BlockSpec((dec_seq, hd), lambda b, h, i: (sbase + b, COL_VD + h)),
            pl.BlockSpec((None, None, past, hd), lambda b, h, i: (b, layer, 0, h)),
            pl.BlockSpec((None, None, past, hd), lambda b, h, i: (b, layer, 0, h)),
            pl.BlockSpec((tq, hd), lambda b, h, i: (i, 0)),
            pl.BlockSpec((tq, hd), lambda b, h, i: (i, 0)),
            pl.BlockSpec((dec_seq, hd), lambda b, h, i: (0, 0)),
            pl.BlockSpec((dec_seq, hd), lambda b, h, i: (0, 0)),
            pl.BlockSpec(memory_space=pl.ANY),
        ],
        out_specs=pl.BlockSpec((tq, hd), lambda b, h, i: (qbase + b * nq + i, h)),
        out_shape=jax.ShapeDtypeStruct((p.shape[0], B_HEADS * hd), BF16),
        input_output_aliases={11: 0},
        scratch_shapes=[pltpu.VMEM((dec_seq + past, hd), BF16), pltpu.VMEM((dec_seq + past, hd), BF16)],
        compiler_params=_cparams("parallel", "parallel", "arbitrary"),
        name="attn_b_lat",
    )(lv, dn_w, p, p, p, ck, cv, cos_d, sin_d, cos_d, sin_d, od_ctx)
    return od_lat


def _expand_heads(a, e3):
    return _dot(jnp.concatenate(_split3(a), axis=1), e3)


def _ssd_direction(xact, dt, a_row, e3, h_sc, hoff, fwd, dskip):
    L = C_CHUNK
    li = lax.broadcasted_iota(jnp.int32, (L, L), 0)
    si = lax.broadcasted_iota(jnp.int32, (L, L), 1)
    tri = (li >= si) if fwd else (li <= si)
    tri_b = jnp.where(tri, 1.0, 0.0).astype(BF16)
    dta = dt * a_row
    cum = _dot(jnp.concatenate([tri_b, tri_b, tri_b], axis=1), jnp.concatenate(_split3(dta), axis=0))
    cum_t = cum.T
    dt_t = dt.T
    end = cum[L - 1:L, :] if fwd else cum[0:1, :]
    to_end = jnp.exp(end - cum) * dt
    ecum = jnp.exp(cum)
    x = xact[:, :C_INNER]
    x_te = (x * _expand_heads(to_end, e3)).astype(BF16)
    dec_row = _expand_heads(jnp.broadcast_to(jnp.exp(end), (8, LANES)), e3)[0:1, :]
    lane = lax.broadcasted_iota(jnp.int32, (L, LANES), 1)
    lo = lane < C_HEAD_DIM
    gw = C_INNER // C_GROUPS
    ys = []
    for g in range(C_GROUPS):
        bm = xact[:, C_INNER + g * C_STATE:C_INNER + (g + 1) * C_STATE]
        cm = xact[:, C_INNER + (C_GROUPS + g) * C_STATE:C_INNER + (C_GROUPS + g + 1) * C_STATE]
        cb = _dot_nt(cm.astype(BF16), bm.astype(BF16))
        h_prev = h_sc[:, g * gw:(g + 1) * gw]
        for pair in range(gw // LANES):
            blk = g * (gw // LANES) + pair
            w_parts, c_parts = [], []
            for j in range(2):
                c = hoff + 2 * blk + j
                seg = cum[:, c:c + 1] - cum_t[c:c + 1, :]
                w = cb * jnp.exp(jnp.where(tri, seg, NEG)) * dt_t[c:c + 1, :]
                w_parts.append(w.astype(BF16))
                c_parts.append((cm * ecum[:, c:c + 1]).astype(BF16))
            xb = x[:, blk * LANES:(blk + 1) * LANES]
            hb = h_prev[:, pair * LANES:(pair + 1) * LANES]
            lhs = jnp.concatenate(w_parts + c_parts, axis=1)
            rhs = jnp.concatenate([jnp.where(lo, xb, 0.0), jnp.where(lo, 0.0, xb),
                                   jnp.where(lo, hb, 0.0), jnp.where(lo, 0.0, hb)], axis=0).astype(BF16)
            ys.append(_dot(lhs, rhs))
        st = _dot(bm.T.astype(BF16), x_te[:, g * gw:(g + 1) * gw])
        h_sc[:, g * gw:(g + 1) * gw] = dec_row[:, g * gw:(g + 1) * gw] * h_prev + st
    y = jnp.concatenate(ys, axis=1)
    if dskip is not None:
        y = y + dskip * x
    return y


def _conv_kernel(x_ref, xp_ref, xn_ref, cw_ref, cbias_ref, o_ref, *, n_ctx_tiles, ctx_tps, lat_tps):
    t = pl.program_id(0)
    is_ctx = t < n_ctx_tiles
    pos = jnp.where(is_ctx, t % ctx_tps, (t - n_ctx_tiles) % lat_tps)
    tps = jnp.where(is_ctx, ctx_tps, lat_tps)
    rows = x_ref.shape[0]
    xp = jnp.concatenate([jnp.where(pos > 0, xp_ref[...], 0.0), x_ref[...],
                          jnp.where(pos < tps - 1, xn_ref[...], 0.0)], axis=0)
    n = rows + 2 * HALO
    acc = jnp.broadcast_to(cbias_ref[...], (rows, C_CONV_CH))
    for k in range(C_CONV):
        shifted = xp if k == C_CONV // 2 else pltpu.roll(xp, (C_CONV // 2 - k) % n, axis=0)
        acc = acc + shifted[HALO:HALO + rows] * cw_ref[k:k + 1, :]
    o_ref[...] = acc * _sigmoid(acc)


def _conv_call(p, conv_w, conv_b, dims, rows=256):
    batch, seq, dec_batch, dec_seq = dims
    t = p.shape[0]
    rows = min(rows, seq)
    assert seq % rows == 0 and dec_seq % rows == 0
    hpt = rows // HALO
    last_halo = t // HALO - 1
    return pl.pallas_call(
        functools.partial(_conv_kernel, n_ctx_tiles=batch * seq // rows, ctx_tps=seq // rows,
                          lat_tps=dec_seq // rows),
        grid=(t // rows,),
        in_specs=[
            pl.BlockSpec((rows, C_CONV_CH), lambda i: (i, COL_XBC)),
            pl.BlockSpec((HALO, C_CONV_CH), lambda i: (jnp.maximum(i * hpt - 1, 0), COL_XBC)),
            pl.BlockSpec((HALO, C_CONV_CH), lambda i: (jnp.minimum((i + 1) * hpt, last_halo), COL_XBC)),
            pl.BlockSpec((C_CONV, C_CONV_CH), lambda i: (0, 0)),
            pl.BlockSpec((1, C_CONV_CH), lambda i: (0, 0)),
        ],
        out_specs=pl.BlockSpec((rows, C_CONV_CH), lambda i: (i, 0)),
        out_shape=jax.ShapeDtypeStruct((t, C_CONV_CH), F32),
        compiler_params=_cparams("parallel"),
        name="conv",
    )(p, p, p, conv_w, conv_b)


def _ssd_kernel(xf_ref, xb_ref, dtf_ref, dtb_ref,
                dtbias_ref, alog_ref, dskip_ref, e3_ref, h0f_ref, h0b_ref,
                yf_ref, yb_ref, hfo_ref, hbo_ref, hf_sc, hb_sc,
                *, n_ctx_chunks, ctx_cps, lat_cps):
    s = pl.program_id(0)
    is_ctx = s < n_ctx_chunks
    pos = jnp.where(is_ctx, s % ctx_cps, (s - n_ctx_chunks) % lat_cps)
    cps = jnp.where(is_ctx, ctx_cps, lat_cps)
    first = pos == 0
    last = pos == cps - 1

    @pl.when(first & is_ctx)
    def _():
        hf_sc[...] = jnp.zeros_like(hf_sc)
        hb_sc[...] = jnp.zeros_like(hb_sc)

    @pl.when(first & jnp.logical_not(is_ctx))
    def _():
        hf_sc[...] = h0f_ref[...].T
        hb_sc[...] = h0b_ref[...].T

    def softplus(v):
        return jnp.maximum(v, 0.0) + jnp.log(1.0 + jnp.exp(-jnp.abs(v)))

    a_row = -jnp.exp(alog_ref[...])
    e3 = e3_ref[...]

    dt = softplus(dtf_ref[...] + dtbias_ref[...])
    yf_ref[...] = _ssd_direction(xf_ref[...], dt, a_row, e3[0], hf_sc, 0, True, dskip_ref[...])

    dt = softplus(dtb_ref[...] + dtbias_ref[...])
    yb_ref[...] = _ssd_direction(xb_ref[...], dt, a_row, e3[1], hb_sc, C_HEADS, False, None)

    @pl.when(last & is_ctx)
    def _():
        hfo_ref[...] = hf_sc[...].T
        hbo_ref[...] = hb_sc[...].T


def _ssd_call(xact, dt, dtbias, alog, dskip_e, e3, h0f, h0b, layer, dims):
    batch, seq, dec_batch, dec_seq = dims
    L = C_CHUNK
    ctx_cps, lat_cps = seq // L, dec_seq // L
    n_ctx_chunks = batch * ctx_cps
    n_chunks = n_ctx_chunks + dec_batch * lat_cps
    t = xact.shape[0]

    def mirror(s):
        c_ctx = (s // ctx_cps) * ctx_cps + (ctx_cps - 1 - s % ctx_cps)
        r = s - n_ctx_chunks
        c_lat = n_ctx_chunks + (r // lat_cps) * lat_cps + (lat_cps - 1 - r % lat_cps)
        return jnp.where(s < n_ctx_chunks, c_ctx, c_lat)

    def lat_b(s):
        return jnp.maximum(s - n_ctx_chunks, 0) // lat_cps

    def ctx_b(s):
        return jnp.minimum(s // ctx_cps, batch - 1)

    const2 = lambda s: (0, 0)
    hspec = pl.BlockSpec((None, None, C_INNER, C_STATE), lambda s: (lat_b(s), layer, 0, 0))
    ospec = pl.BlockSpec((None, C_INNER, C_STATE), lambda s: (ctx_b(s), 0, 0))
    return pl.pallas_call(
        functools.partial(_ssd_kernel, n_ctx_chunks=n_ctx_chunks, ctx_cps=ctx_cps, lat_cps=lat_cps),
        grid=(n_chunks,),
        in_specs=[
            pl.BlockSpec((L, C_CONV_CH), lambda s: (s, 0)),
            pl.BlockSpec((L, C_CONV_CH), lambda s: (mirror(s), 0)),
            pl.BlockSpec((L, LANES), lambda s: (s, 0)),
            pl.BlockSpec((L, LANES), lambda s: (mirror(s), 0)),
            pl.BlockSpec((1, LANES), const2),
            pl.BlockSpec((1, LANES), const2),
            pl.BlockSpec((1, C_INNER), const2),
            pl.BlockSpec((2, 3 * LANES, C_INNER), lambda s: (0, 0, 0)),
            hspec, hspec,
        ],
        out_specs=[
            pl.BlockSpec((L, C_INNER), lambda s: (s, 0)),
            pl.BlockSpec((L, C_INNER), lambda s: (mirror(s), 0)),
            ospec, ospec,
        ],
        out_shape=[
            jax.ShapeDtypeStruct((t, C_INNER), F32), jax.ShapeDtypeStruct((t, C_INNER), F32),
            jax.ShapeDtypeStruct((batch, C_INNER, C_STATE), F32),
            jax.ShapeDtypeStruct((batch, C_INNER, C_STATE), F32),
        ],
        scratch_shapes=[
            pltpu.VMEM((C_STATE, C_INNER), F32),
            pltpu.VMEM((C_STATE, C_INNER), F32),
        ],
        compiler_params=_cparams("arbitrary"),
        name="ssd",
    )(xact, xact, dt, dt, dtbias, alog, dskip_e, e3, h0f, h0b)


def _outproj_kernel(oa_ref, od_ref, yf_ref, yb_ref, z_ref, sn_ref, h_ref, mod_ref, w_ref, o_ref, mix_sc):
    j = pl.program_id(1)
    tn = o_ref.shape[1]

    @pl.when(j == 0)
    def _():
        na = oa_ref.shape[1]
        nd = od_ref.shape[1]
        mix_sc[:, 0:na] = oa_ref[...]
        mix_sc[:, na:na + nd] = od_ref[...]
        z = z_ref[...]
        y = (yf_ref[...] + yb_ref[...]) * (z * _sigmoid(z))
        gw = C_INNER // C_GROUPS
        for g in range(C_GROUPS):
            yg = y[:, g * gw:(g + 1) * gw]
            yg = yg * lax.rsqrt(jnp.mean(yg * yg, axis=-1, keepdims=True) + EPS)
            yg = yg * sn_ref[:, g * gw:(g + 1) * gw]
            mix_sc[:, na + nd + g * gw:na + nd + (g + 1) * gw] = yg.astype(BF16)

    gate = mod_ref[2:3, pl.ds(pl.multiple_of(j * tn, tn), tn)]
    o_ref[...] = h_ref[...] + gate * _dot(mix_sc[...], w_ref[...])


def _outproj_call(oa, od, yf, yb, p, ssm_norm, h, mod_l, w, mod_row, tm, tn=1024):
    t, d = h.shape
    tn = min(tn, d)
    mw = w.shape[0]
    zblk = C_INNER // LANES
    return pl.pallas_call(
        _outproj_kernel,
        grid=(t // tm, d // tn),
        in_specs=[
            pl.BlockSpec((tm, oa.shape[1]), lambda i, j: (i, 0)),
            pl.BlockSpec((tm, od.shape[1]), lambda i, j: (i, 0)),
            pl.BlockSpec((tm, C_INNER), lambda i, j: (i, 0)),
            pl.BlockSpec((tm, C_INNER), lambda i, j: (i, 0)),
            pl.BlockSpec((tm, C_INNER), lambda i, j: (i, COL_Z // zblk)),
            pl.BlockSpec((1, C_INNER), lambda i, j: (0, 0)),
            pl.BlockSpec((tm, tn), lambda i, j: (i, j)),
            pl.BlockSpec((None, N_MOD, d), lambda i, j: (mod_row(i * tm), 0, 0)),
            pl.BlockSpec((mw, tn), lambda i, j: (0, j)),
        ],
        out_specs=pl.BlockSpec((tm, tn), lambda i, j: (i, j)),
        out_shape=jax.ShapeDtypeStruct((t, d), F32),
        scratch_shapes=[pltpu.VMEM((tm, mw), BF16)],
        compiler_params=_cparams("parallel", "arbitrary"),
        name="outproj",
    )(oa, od, yf, yb, p, ssm_norm, h, mod_l, w)


def _ffn_kernel(h_ref, mod_ref, nw_ref, wg_ref, wu_ref, wd_ref, o_ref, u_sc, acc_sc):
    f = pl.program_id(1)

    @pl.when(f == 0)
    def _():
        u_sc[...] = _modnorm(h_ref[...], nw_ref[...], mod_ref[3:4, :], mod_ref[4:5, :]).astype(BF16)
        acc_sc[...] = jnp.zeros_like(acc_sc)

    u = u_sc[...]
    g = _dot(u, wg_ref[...])
    a = (g * _sigmoid(g)) * _dot(u, wu_ref[...])
    acc_sc[...] += _dot(a.astype(BF16), wd_ref[...])

    @pl.when(f == pl.num_programs(1) - 1)
    def _():
        o_ref[...] = h_ref[...] + mod_ref[5:6, :] * acc_sc[...]


def _ffn_call(h, mod_l, nw, w_gu, w_d, mod_row, tm, tf=512):
    t, d = h.shape
    ff = w_d.shape[0]
    nf = ff // tf
    return pl.pallas_call(
        _ffn_kernel,
        grid=(t // tm, nf),
        in_specs=[
            pl.BlockSpec((tm, d), lambda i, f: (i, 0)),
            pl.BlockSpec((None, N_MOD, d), lambda i, f: (mod_row(i * tm), 0, 0)),
            pl.BlockSpec((1, d), lambda i, f: (0, 0)),
            pl.BlockSpec((d, tf), lambda i, f: (0, f)),
            pl.BlockSpec((d, tf), lambda i, f: (0, nf + f)),
            pl.BlockSpec((tf, d), lambda i, f: (f, 0)),
        ],
        out_specs=pl.BlockSpec((tm, d), lambda i, f: (i, 0)),
        out_shape=jax.ShapeDtypeStruct((t, d), F32),
        scratch_shapes=[pltpu.VMEM((tm, d), BF16), pltpu.VMEM((tm, d), F32)],
        compiler_params=_cparams("parallel", "arbitrary"),
        name="ffn",
    )(h, mod_l, nw, w_gu, w_gu, w_d)


def _final_norm_kernel(h_ref, w_ref, oc_ref, ol_ref, *, n_ctx_tiles):
    x = h_ref[...]
    y = x * lax.rsqrt(jnp.mean(x * x, axis=-1, keepdims=True) + EPS) * w_ref[...]
    i = pl.program_id(0)

    @pl.when(i < n_ctx_tiles)
    def _():
        oc_ref[...] = y

    @pl.when(i >= n_ctx_tiles)
    def _():
        ol_ref[...] = y


def _final_norm_call(h, w, n_ctx, tm):
    t, d = h.shape
    nc = n_ctx // tm
    return pl.pallas_call(
        functools.partial(_final_norm_kernel, n_ctx_tiles=nc),
        grid=(t // tm,),
        in_specs=[pl.BlockSpec((tm, d), lambda i: (i, 0)), pl.BlockSpec((1, d), lambda i: (0, 0))],
        out_specs=[pl.BlockSpec((tm, d), lambda i: (jnp.minimum(i, nc - 1), 0)),
                   pl.BlockSpec((tm, d), lambda i: (jnp.maximum(i - nc, 0), 0))],
        out_shape=[jax.ShapeDtypeStruct((n_ctx, d), F32), jax.ShapeDtypeStruct((t - n_ctx, d), F32)],
        compiler_params=_cparams("arbitrary"),
        name="final_norm",
    )(h, w)


def _rope_tables(n, rot_dim):
    rows = n // GRID_W
    row = jnp.repeat(jnp.arange(rows), GRID_W).astype(F32)
    col = (jnp.arange(rows * GRID_W) % GRID_W).astype(F32)
    quarter = rot_dim // 4
    inv = ROPE_BASE ** (-jnp.arange(quarter, dtype=F32) / quarter)
    ang = jnp.concatenate([row[:, None] * inv, col[:, None] * inv], axis=-1)
    c, s = jnp.cos(ang), jnp.sin(ang)
    reps = LANES // rot_dim
    return (jnp.tile(jnp.concatenate([c, c], axis=-1), (1, reps)),
            jnp.tile(jnp.concatenate([-s, s], axis=-1), (1, reps)))


def _lambda_init(layer):
    return 0.8 - 0.6 * math.exp(-0.3 * layer)


def _permute_w_in(w):
    sizes = (A_HEADS * HEAD_DIM, A_KV_HEADS * HEAD_DIM, A_KV_HEADS * HEAD_DIM,
             B_HEADS * HEAD_DIM, B_HEADS * HEAD_DIM, B_HEADS * HEAD_DIM, C_INNER, C_CONV_CH, DT_COLS)
    offs = [0]
    for sz in sizes:
        offs.append(offs[-1] + sz)
    qa, ka, va, qd, kd, vd, z, xbc, dt = [w[..., offs[i]:offs[i + 1]] for i in range(len(sizes))]
    main = jnp.concatenate([xbc, qa, z, ka, va, qd, kd, vd], axis=-1).astype(BF16)
    dtw = jnp.pad(dt, ((0, 0), (0, 0), (0, LANES - DT_COLS))).astype(BF16)
    return main, dtw


def _pad_lanes(v):
    return jnp.pad(v.reshape(v.shape[0], 1, -1), ((0, 0), (0, 0), (0, LANES - DT_COLS)))


def kernel(x_prompt, x_sample, cache_attn_k, cache_attn_v, cache_diff_k, cache_diff_v, state_ssm_fwd, state_ssm_bwd, c, c_ctx, w_ada, b_ada, norm_mix, norm_ffn, w_in, attn_sink, diff_lambda, diff_norm, conv_w, conv_b, dt_bias, a_log, d_skip, ssm_norm, w_out, w_gate_up, w_down, norm_final):
    batch, seq, d = x_prompt.shape
    dec_batch, dec_seq, _ = x_sample.shape
    depth = w_in.shape[0]
    past = cache_attn_k.shape[2]
    n_ctx = batch * seq
    dims = (batch, seq, dec_batch, dec_seq)
    tm = 512 if (n_ctx % 512 == 0 and dec_seq % 512 == 0) else 256
    tq = min(256, dec_seq)
    assert n_ctx % dec_seq == 0 and n_ctx % tm == 0 and dec_seq % tm == 0
    assert seq % C_CHUNK == 0 and dec_seq % C_CHUNK == 0 and dec_seq % GRID_W == 0
    assert 1 + dec_batch <= 8

    def mod_row(start):
        return jnp.where(start < n_ctx, 0, 1 + (start - n_ctx) // dec_seq)

    w_in_main, w_in_dt = _permute_w_in(w_in)
    w_out_b = w_out.astype(BF16)
    w_gu_b = w_gate_up.astype(BF16)
    w_down_b = w_down.astype(BF16)
    cos_a, sin_a = _rope_tables(dec_seq, HEAD_DIM)
    cos_d, sin_d = _rope_tables(dec_seq, B_HALF)
    dtbias = _pad_lanes(dt_bias)
    alog = _pad_lanes(a_log)
    dskip_e = jnp.repeat(d_skip, C_HEAD_DIM, axis=-1).reshape(depth, 1, C_INNER)
    head_of_lane = jnp.arange(C_INNER) // C_HEAD_DIM
    e_f = (jnp.arange(LANES)[:, None] == head_of_lane[None, :]).astype(BF16)
    e_b = (jnp.arange(LANES)[:, None] == head_of_lane[None, :] + C_HEADS).astype(BF16)
    e3 = jnp.stack([jnp.concatenate([e_f] * 3, axis=0), jnp.concatenate([e_b] * 3, axis=0)])
    ck_a = cache_attn_k.reshape(dec_batch, depth, past, A_KV_HEADS * HEAD_DIM)
    cv_a = cache_attn_v.reshape(dec_batch, depth, past, A_KV_HEADS * HEAD_DIM)
    ck_d = cache_diff_k.reshape(dec_batch, depth, past, B_HEADS * HEAD_DIM)
    cv_d = cache_diff_v.reshape(dec_batch, depth, past, B_HEADS * HEAD_DIM)
    h0f = state_ssm_fwd.reshape(dec_batch, depth, C_INNER, C_STATE)
    h0b = state_ssm_bwd.reshape(dec_batch, depth, C_INNER, C_STATE)

    cond8 = jnp.concatenate([c_ctx[None, :], c, jnp.zeros((8 - 1 - dec_batch, d), F32)], axis=0)
    mod = _ada_call(cond8, w_ada, b_ada)[:, :1 + dec_batch].reshape(depth, 1 + dec_batch, N_MOD, d)

    h = jnp.concatenate([x_prompt.reshape(n_ctx, d), x_sample.reshape(dec_batch * dec_seq, d)], axis=0)
    ctx_out = []
    for l in range(depth):
        lam_init = _lambda_init(l)
        p, dt = _inproj_call(h, mod[l], norm_mix[l][None, :], w_in_main[l], w_in_dt[l], mod_row, tm)
        oa = _attn_a_calls(p, attn_sink, ck_a, cv_a, cos_a, sin_a, l, dims)
        od = _attn_b_calls(p, diff_lambda[l], diff_norm[l][None, :], ck_d, cv_d, cos_d, sin_d,
                           l, lam_init, dims, tq)
        xact = _conv_call(p, conv_w[l], conv_b[l][None, :], dims)
        yf, yb, hf, hb = _ssd_call(xact, dt, dtbias[l], alog[l], dskip_e[l], e3, h0f, h0b, l, dims)
        h = _outproj_call(oa, od, yf, yb, p, ssm_norm[l][None, :], h, mod[l], w_out_b[l], mod_row, tm)
        h = _ffn_call(h, mod[l], norm_ffn[l][None, :], w_gu_b[l], w_down_b[l], mod_row, tm)

        def ctx_cols(lo, hi, heads):
            return p[:n_ctx, lo * LANES:hi * LANES].reshape(batch, seq, heads, HEAD_DIM)

        ctx_out.append((ctx_cols(COL_KA, COL_VA, A_KV_HEADS), ctx_cols(COL_VA, COL_QD, A_KV_HEADS),
                        ctx_cols(COL_KD, COL_VD, B_HEADS), ctx_cols(COL_VD, P_COLS // LANES, B_HEADS),
                        hf.reshape(batch, C_HEADS, C_HEAD_DIM, C_STATE),
                        hb.reshape(batch, C_HEADS, C_HEAD_DIM, C_STATE)))

    y_prompt, y_sample = _final_norm_call(h, norm_final[None, :], n_ctx, tm)
    stacked = [jnp.stack([t[i] for t in ctx_out], axis=1) for i in range(6)]
    return (y_prompt.reshape(batch, seq, d), y_sample.reshape(dec_batch, dec_seq, d), *stacked)
```

```python
import functools
import math

import jax
import jax.numpy as jnp
from jax import lax
from jax.experimental import pallas as pl
from jax.experimental.pallas import tpu as pltpu

F32 = jnp.float32
BF16 = jnp.bfloat16

HEAD_DIM = 128
A_HEADS = 4
A_KV_HEADS = 2
A_GROUP = A_HEADS // A_KV_HEADS
BLOCK = 128
B_HEADS = 4
B_HALF = HEAD_DIM // 2
C_HEADS = 16
C_HEAD_DIM = 64
C_INNER = C_HEADS * C_HEAD_DIM
C_GROUPS = 2
C_STATE = 128
C_CONV = 5
C_CHUNK = 128
C_CONV_CH = C_INNER + 2 * C_GROUPS * C_STATE
GRID_W = 64
EPS = 1e-6
ROPE_BASE = 10000.0
N_MOD = 6
LANES = 128
HALO = 8
NEG = -1e30
LOG2E = math.log2(math.e)

COL_QA, COL_KA, COL_VA, COL_QD, COL_KD, COL_VD, COL_Z, COL_XBC = 0, 4, 6, 8, 12, 16, 20, 28
P_COLS = 40 * LANES
HALF_INNER = C_INNER // C_GROUPS
DT_COLS = 2 * C_HEADS

VMEM_LIMIT = 48 * 1024 * 1024


def _cparams(*sem):
    return pltpu.CompilerParams(dimension_semantics=sem, vmem_limit_bytes=VMEM_LIMIT)


def _dot(a, b):
    return jnp.dot(a, b, preferred_element_type=F32)


def _dot_nt(a, b):
    return lax.dot_general(a, b, (((1,), (1,)), ((), ())), preferred_element_type=F32)


def _sigmoid(x):
    return 1.0 / (1.0 + jnp.exp(-x))


def _split3(a):
    hi = a.astype(BF16)
    r = a - hi.astype(F32)
    mid = r.astype(BF16)
    lo = (r - mid.astype(F32)).astype(BF16)
    return hi, mid, lo


def _ada_kernel(cond_ref, w_ref, b_ref, o_ref):
    s = cond_ref[...]
    s = s * _sigmoid(s)
    o_ref[...] = _dot(s.astype(BF16), w_ref[...].astype(BF16)) + b_ref[...]


def _ada_call(cond8, w_ada, b_ada, tn=1024):
    depth, d, n = w_ada.shape
    return pl.pallas_call(
        _ada_kernel,
        grid=(depth, n // tn),
        in_specs=[
            pl.BlockSpec((8, d), lambda l, j: (0, 0)),
            pl.BlockSpec((None, d, tn), lambda l, j: (l, 0, j)),
            pl.BlockSpec((None, 1, tn), lambda l, j: (l, 0, j)),
        ],
        out_specs=pl.BlockSpec((None, 8, tn), lambda l, j: (l, 0, j)),
        out_shape=jax.ShapeDtypeStruct((depth, 8, n), F32),
        compiler_params=_cparams("parallel", "parallel"),
        name="adaln",
    )(cond8, w_ada, b_ada.reshape(depth, 1, n))


def _modnorm(x, nw, shift, scale):
    ms = jnp.mean(x * x, axis=-1, keepdims=True)
    y = x * lax.rsqrt(ms + EPS) * nw
    return y * (1.0 + scale) + shift


def _inproj_kernel(h_ref, mod_ref, nw_ref, w_ref, wdt_ref, p_ref, dt_ref, u_sc):
    @pl.when(pl.program_id(1) == 0)
    def _():
        u = _modnorm(h_ref[...], nw_ref[...], mod_ref[0:1, :], mod_ref[1:2, :]).astype(BF16)
        u_sc[...] = u
        dt_ref[...] = _dot(u, wdt_ref[...])

    p_ref[...] = _dot(u_sc[...], w_ref[...])


def _inproj_call(h, mod_l, nw, w, wdt, layer, mod_row, tm, tn=1024):
    t, d = h.shape
    return pl.pallas_call(
        _inproj_kernel,
        grid=(t // tm, P_COLS // tn),
        in_specs=[
            pl.BlockSpec((tm, d), lambda i, j: (i, 0)),
            pl.BlockSpec((None, N_MOD, d), lambda i, j: (mod_row(i * tm), 0, 0)),
            pl.BlockSpec((1, d), lambda i, j: (0, 0)),
            pl.BlockSpec((None, d, tn), lambda i, j: (layer, 0, j)),
            pl.BlockSpec((None, d, LANES), lambda i, j: (layer, 0, 0)),
        ],
        out_specs=[
            pl.BlockSpec((tm, tn), lambda i, j: (i, j)),
            pl.BlockSpec((tm, LANES), lambda i, j: (i, 0)),
        ],
        out_shape=[jax.ShapeDtypeStruct((t, P_COLS), F32), jax.ShapeDtypeStruct((t, LANES), F32)],
        scratch_shapes=[pltpu.VMEM((tm, d), BF16)],
        compiler_params=_cparams("parallel", "arbitrary"),
        name="inproj",
    )(h, mod_l, nw, w, wdt)


def _rope_a(x, c, s):
    return x * c + pltpu.roll(x, HEAD_DIM // 2, axis=1) * s


def _rope_d(x, c, s):
    lane = lax.broadcasted_iota(jnp.int32, x.shape, 1)
    q = B_HALF // 2
    partner = jnp.where((lane & (B_HALF - 1)) < q, pltpu.roll(x, LANES - q, axis=1), pltpu.roll(x, q, axis=1))
    return x * c + partner * s


def _sink_softmax_pv(s, sink, v):
    m = jnp.maximum(jnp.max(s, axis=-1, keepdims=True), sink)
    e = jnp.exp2(s - m)
    den = jnp.sum(e, axis=-1, keepdims=True) + jnp.exp2(sink - m)
    return _dot(e.astype(BF16), v) / den


def _attn_a_ctx_kernel(sink_ref, q_ref, k_ref, v_ref, o_ref, *, layer):
    kv = pl.program_id(1)
    k = k_ref[...].astype(BF16)
    v = v_ref[...].astype(BF16)
    scale = HEAD_DIM ** -0.5 * LOG2E
    for g in range(A_GROUP):
        q = (q_ref[:, g * HEAD_DIM:(g + 1) * HEAD_DIM] * scale).astype(BF16)
        s = _dot_nt(q, k)
        o = _sink_softmax_pv(s, sink_ref[layer, kv * A_GROUP + g] * LOG2E, v)
        o_ref[:, g * HEAD_DIM:(g + 1) * HEAD_DIM] = o.astype(o_ref.dtype)


def _attn_a_lat_kernel(sink_ref, q_ref, kp_ref, kc_ref, kn_ref, vp_ref, vc_ref, vn_ref,
                       kx_ref, vx_ref, cq_ref, sq_ref, cp_ref, sp_ref, cn_ref, sn_ref,
                       ctx_rows_ref, o_ref, *, layer, n_blocks):
    del ctx_rows_ref
    nb = pl.program_id(1)
    scale = HEAD_DIM ** -0.5 * LOG2E
    cq, sq = cq_ref[...], sq_ref[...]
    cp, sp, cn, sn = cp_ref[...], sp_ref[...], cn_ref[...], sn_ref[...]
    n_ctx = kx_ref.shape[0]
    qi = lax.broadcasted_iota(jnp.int32, (BLOCK, 3 * BLOCK + n_ctx), 0)
    kj = lax.broadcasted_iota(jnp.int32, (BLOCK, 3 * BLOCK + n_ctx), 1)
    bad_prev = (kj < BLOCK) & ((kj < qi) | (nb == 0))
    bad_next = (kj >= 2 * BLOCK) & (kj < 3 * BLOCK) & ((kj - 2 * BLOCK > qi) | (nb == n_blocks - 1))
    mask = jnp.logical_not(bad_prev | bad_next)
    for kv in range(A_KV_HEADS):
        ks = slice(kv * HEAD_DIM, (kv + 1) * HEAD_DIM)
        k_all = jnp.concatenate([
            _rope_a(kp_ref[:, ks], cp, sp).astype(BF16),
            _rope_a(kc_ref[:, ks], cq, sq).astype(BF16),
            _rope_a(kn_ref[:, ks], cn, sn).astype(BF16),
            kx_ref[:, ks].astype(BF16)], axis=0)
        v_all = jnp.concatenate([vp_ref[:, ks], vc_ref[:, ks], vn_ref[:, ks], vx_ref[:, ks]],
                                axis=0).astype(BF16)
        for g in range(A_GROUP):
            head = kv * A_GROUP + g
            sl = slice(head * HEAD_DIM, (head + 1) * HEAD_DIM)
            q = (_rope_a(q_ref[:, sl], cq, sq) * scale).astype(BF16)
            s = jnp.where(mask, _dot_nt(q, k_all), NEG)
            o = _sink_softmax_pv(s, sink_ref[layer, head] * LOG2E, v_all)
            o_ref[:, sl] = o.astype(o_ref.dtype)


def _attn_a_calls(p, sink, ck, cv, cos_a, sin_a, layer, dims):
    t = p.shape[0]
    batch, seq, dec_batch, dec_seq = dims
    n_ctx = batch * seq
    smem = pl.BlockSpec(memory_space=pltpu.SMEM)
    qw = A_GROUP * HEAD_DIM
    oa_ctx = pl.pallas_call(
        functools.partial(_attn_a_ctx_kernel, layer=layer),
        grid=(batch, A_KV_HEADS),
        in_specs=[
            smem,
            pl.BlockSpec((seq, qw), lambda b, k: (b, COL_QA // A_GROUP + k)),
            pl.BlockSpec((seq, HEAD_DIM), lambda b, k: (b, COL_KA + k)),
            pl.BlockSpec((seq, HEAD_DIM), lambda b, k: (b, COL_VA + k)),
        ],
        out_specs=pl.BlockSpec((seq, qw), lambda b, k: (b, k)),
        out_shape=jax.ShapeDtypeStruct((t, A_HEADS * HEAD_DIM), BF16),
        compiler_params=_cparams("parallel", "parallel"),
        name="attn_a_ctx",
    )(sink, p, p, p)

    nbl = dec_seq // BLOCK
    base = n_ctx // BLOCK

    def row(b, nb):
        return base + b * nbl + nb

    def prev(nb):
        return jnp.maximum(nb - 1, 0)

    def nxt(nb):
        return jnp.minimum(nb + 1, nbl - 1)

    past = ck.shape[2]
    aw = A_HEADS * HEAD_DIM
    kw = A_KV_HEADS * HEAD_DIM
    qcol, kcol, vcol = COL_QA * LANES // aw, COL_KA * LANES // kw, COL_VA * LANES // kw
    kblk = (BLOCK, kw)
    tblk = (BLOCK, HEAD_DIM)
    oa_lat = pl.pallas_call(
        functools.partial(_attn_a_lat_kernel, layer=layer, n_blocks=nbl),
        grid=(dec_batch, nbl),
        in_specs=[
            smem,
            pl.BlockSpec((BLOCK, aw), lambda b, n: (row(b, n), qcol)),
            pl.BlockSpec(kblk, lambda b, n: (row(b, prev(n)), kcol)),
            pl.BlockSpec(kblk, lambda b, n: (row(b, n), kcol)),
            pl.BlockSpec(kblk, lambda b, n: (row(b, nxt(n)), kcol)),
            pl.BlockSpec(kblk, lambda b, n: (row(b, prev(n)), vcol)),
            pl.BlockSpec(kblk, lambda b, n: (row(b, n), vcol)),
            pl.BlockSpec(kblk, lambda b, n: (row(b, nxt(n)), vcol)),
            pl.BlockSpec((None, None, past, kw), lambda b, n: (b, layer, 0, 0)),
            pl.BlockSpec((None, None, past, kw), lambda b, n: (b, layer, 0, 0)),
            pl.BlockSpec(tblk, lambda b, n: (n, 0)),
            pl.BlockSpec(tblk, lambda b, n: (n, 0)),
            pl.BlockSpec(tblk, lambda b, n: (prev(n), 0)),
            pl.BlockSpec(tblk, lambda b, n: (prev(n), 0)),
            pl.BlockSpec(tblk, lambda b, n: (nxt(n), 0)),
            pl.BlockSpec(tblk, lambda b, n: (nxt(n), 0)),
            pl.BlockSpec(memory_space=pl.ANY),
        ],
        out_specs=pl.BlockSpec((BLOCK, aw), lambda b, n: (row(b, n), 0)),
        out_shape=jax.ShapeDtypeStruct((t, aw), BF16),
        input_output_aliases={16: 0},
        compiler_params=_cparams("parallel", "parallel"),
        name="attn_a_lat",
    )(sink, p, p, p, p, p, p, p, ck, cv, cos_a, sin_a, cos_a, sin_a, cos_a, sin_a, oa_ctx)
    return oa_lat


def _diff_lambda(lv, lam_init):
    a = jnp.sum(lv[0:1, :] * lv[1:2, :], axis=-1, keepdims=True)
    b = jnp.sum(lv[2:3, :] * lv[3:4, :], axis=-1, keepdims=True)
    return jnp.exp(a) - jnp.exp(b) + lam_init


def _diff_core(q, k, v, lam, dn_w, lam_init):
    lane = lax.broadcasted_iota(jnp.int32, q.shape, 1)
    qs = q * (B_HALF ** -0.5 * LOG2E)
    outs = []
    for half in range(2):
        sel = (lane < B_HALF) if half == 0 else (lane >= B_HALF)
        s = _dot_nt(jnp.where(sel, qs, 0.0).astype(BF16), k)
        e = jnp.exp2(s - jnp.max(s, axis=-1, keepdims=True))
        den = jnp.sum(e, axis=-1, keepdims=True)
        outs.append(_dot(e.astype(BF16), v) / den)
    o = outs[0] - lam * outs[1]
    ms = jnp.mean(o * o, axis=-1, keepdims=True)
    return o * lax.rsqrt(ms + EPS) * dn_w * (1.0 - lam_init)


def _attn_b_ctx_kernel(lv_ref, dn_ref, q_ref, k_ref, v_ref, o_ref, *, lam_init):
    lam = _diff_lambda(lv_ref[...], lam_init)
    o = _diff_core(q_ref[...], k_ref[...].astype(BF16), v_ref[...].astype(BF16), lam, dn_ref[...], lam_init)
    o_ref[...] = o.astype(o_ref.dtype)


def _attn_b_lat_kernel(lv_ref, dn_ref, q_ref, k_ref, v_ref, kx_ref, vx_ref, cq_ref, sq_ref,
                       ck_ref, sk_ref, ctx_rows_ref, o_ref, k_sc, v_sc, *, lam_init):
    del ctx_rows_ref
    n = k_ref.shape[0]

    @pl.when(pl.program_id(2) == 0)
    def _():
        k_sc[0:n, :] = _rope_d(k_ref[...], ck_ref[...], sk_ref[...]).astype(BF16)
        k_sc[n:, :] = kx_ref[...].astype(BF16)
        v_sc[0:n, :] = v_ref[...].astype(BF16)
        v_sc[n:, :] = vx_ref[...].astype(BF16)

    lam = _diff_lambda(lv_ref[...], lam_init)
    q = _rope_d(q_ref[...], cq_ref[...], sq_ref[...])
    o = _diff_core(q, k_sc[...], v_sc[...], lam, dn_ref[...], lam_init)
    o_ref[...] = o.astype(o_ref.dtype)


def _attn_b_calls(p, lv, dn_w, ck, cv, cos_d, sin_d, layer, lam_init, dims, tq):
    batch, seq, dec_batch, dec_seq = dims
    n_ctx = batch * seq
    hd = HEAD_DIM
    od_ctx = pl.pallas_call(
        functools.partial(_attn_b_ctx_kernel, lam_init=lam_init),
        grid=(batch, B_HEADS),
        in_specs=[
            pl.BlockSpec((4, B_HALF), lambda b, h: (0, 0)),
            pl.BlockSpec((1, hd), lambda b, h: (0, 0)),
            pl.BlockSpec((seq, hd), lambda b, h: (b, COL_QD + h)),
            pl.BlockSpec((seq, hd), lambda b, h: (b, COL_KD + h)),
            pl.BlockSpec((seq, hd), lambda b, h: (b, COL_VD + h)),
        ],
        out_specs=pl.BlockSpec((seq, hd), lambda b, h: (b, h)),
        out_shape=jax.ShapeDtypeStruct((p.shape[0], B_HEADS * hd), BF16),
        compiler_params=_cparams("parallel", "parallel"),
        name="attn_b_ctx",
    )(lv, dn_w, p, p, p)

    past = ck.shape[2]
    nq = dec_seq // tq
    qbase = n_ctx // tq
    sbase = n_ctx // dec_seq
    od_lat = pl.pallas_call(
        functools.partial(_attn_b_lat_kernel, lam_init=lam_init),
        grid=(dec_batch, B_HEADS, nq),
        in_specs=[
            pl.BlockSpec((4, B_HALF), lambda b, h, i: (0, 0)),
            pl.BlockSpec((1, hd), lambda b, h, i: (0, 0)),
            pl.BlockSpec((tq, hd), lambda b, h, i: (qbase + b * nq + i, COL_QD + h)),
            pl.BlockSpec((dec_seq, hd), lambda b, h, i: (sbase + b, COL_KD + h)),
            pl.BlockSpec((dec_seq, hd), lambda b, h, i: (sbase + b, COL_VD + h)),
            pl.BlockSpec((None, None, past, hd), lambda b, h, i: (b, layer, 0, h)),
            pl.BlockSpec((None, None, past, hd), lambda b, h, i: (b, layer, 0, h)),
            pl.BlockSpec((tq, hd), lambda b, h, i: (i, 0)),
            pl.BlockSpec((tq, hd), lambda b, h, i: (i, 0)),
            pl.BlockSpec((dec_seq, hd), lambda b, h, i: (0, 0)),
            pl.BlockSpec((dec_seq, hd), lambda b, h, i: (0, 0)),
            pl.BlockSpec(memory_space=pl.ANY),
        ],
        out_specs=pl.BlockSpec((tq, hd), lambda b, h, i: (qbase + b * nq + i, h)),
        out_shape=jax.ShapeDtypeStruct((p.shape[0], B_HEADS * hd), BF16),
        input_output_aliases={11: 0},
        scratch_shapes=[pltpu.VMEM((dec_seq + past, hd), BF16), pltpu.VMEM((dec_seq + past, hd), BF16)],
        compiler_params=_cparams("parallel", "parallel", "arbitrary"),
        name="attn_b_lat",
    )(lv, dn_w, p, p, p, ck, cv, cos_d, sin_d, cos_d, sin_d, od_ctx)
    return od_lat


def _expand_heads(a, e3):
    return _dot(jnp.concatenate(_split3(a), axis=1), e3)


def _ssd_direction(xact, dt, a_row, e3, h_sc, hoff, fwd, dskip):
    L = C_CHUNK
    li = lax.broadcasted_iota(jnp.int32, (L, L), 0)
    si = lax.broadcasted_iota(jnp.int32, (L, L), 1)
    tri = (li >= si) if fwd else (li <= si)
    tri_b = jnp.where(tri, 1.0, 0.0).astype(BF16)
    dta = dt * a_row
    cum = _dot(jnp.concatenate([tri_b, tri_b, tri_b], axis=1), jnp.concatenate(_split3(dta), axis=0))
    cum_t = cum.T
    dt_t = dt.T
    end = cum[L - 1:L, :] if fwd else cum[0:1, :]
    to_end = jnp.exp(end - cum) * dt
    ecum = jnp.exp(cum)
    x = xact[:, :C_INNER]
    x_te = (x * _expand_heads(to_end, e3)).astype(BF16)
    dec_row = _expand_heads(jnp.broadcast_to(jnp.exp(end), (8, LANES)), e3)[0:1, :]
    lane = lax.broadcasted_iota(jnp.int32, (L, LANES), 1)
    lo = lane < C_HEAD_DIM
    gw = C_INNER // C_GROUPS
    ys = []
    for g in range(C_GROUPS):
        bm = xact[:, C_INNER + g * C_STATE:C_INNER + (g + 1) * C_STATE]
        cm = xact[:, C_INNER + (C_GROUPS + g) * C_STATE:C_INNER + (C_GROUPS + g + 1) * C_STATE]
        cb = _dot_nt(cm.astype(BF16), bm.astype(BF16))
        h_prev = h_sc[:, g * gw:(g + 1) * gw]
        for pair in range(gw // LANES):
            blk = g * (gw // LANES) + pair
            w_parts, c_parts = [], []
            for j in range(2):
                c = hoff + 2 * blk + j
                seg = cum[:, c:c + 1] - cum_t[c:c + 1, :]
                w = cb * jnp.exp(jnp.where(tri, seg, NEG)) * dt_t[c:c + 1, :]
                w_parts.append(w.astype(BF16))
                c_parts.append((cm * ecum[:, c:c + 1]).astype(BF16))
            xb = x[:, blk * LANES:(blk + 1) * LANES]
            hb = h_prev[:, pair * LANES:(pair + 1) * LANES]
            lhs = jnp.concatenate(w_parts + c_parts, axis=1)
            rhs = jnp.concatenate([jnp.where(lo, xb, 0.0), jnp.where(lo, 0.0, xb),
                                   jnp.where(lo, hb, 0.0), jnp.where(lo, 0.0, hb)], axis=0).astype(BF16)
            ys.append(_dot(lhs, rhs))
        st = _dot(bm.T.astype(BF16), x_te[:, g * gw:(g + 1) * gw])
        h_sc[:, g * gw:(g + 1) * gw] = dec_row[:, g * gw:(g + 1) * gw] * h_prev + st
    y = jnp.concatenate(ys, axis=1)
    if dskip is not None:
        y = y + dskip * x
    return y


def _conv_kernel(*refs, ncol, n_ctx_tiles, ctx_tps, lat_tps):
    x_refs, xp_refs, xn_refs = refs[:ncol], refs[ncol:2 * ncol], refs[2 * ncol:3 * ncol]
    cw_ref, cbias_ref, o_ref = refs[3 * ncol:]
    t = pl.program_id(0)
    is_ctx = t < n_ctx_tiles
    pos = jnp.where(is_ctx, t % ctx_tps, (t - n_ctx_tiles) % lat_tps)
    tps = jnp.where(is_ctx, ctx_tps, lat_tps)
    rows = o_ref.shape[0]

    def cols(rs):
        return jnp.concatenate([r[...] for r in rs], axis=1)

    xp = jnp.concatenate([jnp.where(pos > 0, cols(xp_refs), 0.0), cols(x_refs),
                          jnp.where(pos < tps - 1, cols(xn_refs), 0.0)], axis=0)
    n = rows + 2 * HALO
    acc = jnp.broadcast_to(cbias_ref[...], o_ref.shape)
    for k in range(C_CONV):
        shifted = xp if k == C_CONV // 2 else pltpu.roll(xp, (C_CONV // 2 - k) % n, axis=0)
        acc = acc + shifted[HALO:HALO + rows] * cw_ref[k:k + 1, :]
    o_ref[...] = acc * _sigmoid(acc)


def _conv_call(p, conv_w, conv_b, layer, dims, rows=512):
    batch, seq, dec_batch, dec_seq = dims
    t = p.shape[0]
    rows = min(rows, seq)
    assert seq % rows == 0 and dec_seq % rows == 0
    hpt = rows // HALO
    last_halo = t // HALO - 1
    cw = HALF_INNER
    c0 = COL_XBC * LANES // cw
    ncol = C_CONV_CH // cw
    main = [pl.BlockSpec((rows, cw), lambda i, c=c: (i, c0 + c)) for c in range(ncol)]
    prev = [pl.BlockSpec((HALO, cw), lambda i, c=c: (jnp.maximum(i * hpt - 1, 0), c0 + c)) for c in range(ncol)]
    nxt = [pl.BlockSpec((HALO, cw), lambda i, c=c: (jnp.minimum((i + 1) * hpt, last_halo), c0 + c))
           for c in range(ncol)]
    return pl.pallas_call(
        functools.partial(_conv_kernel, ncol=ncol, n_ctx_tiles=batch * seq // rows, ctx_tps=seq // rows,
                          lat_tps=dec_seq // rows),
        grid=(t // rows,),
        in_specs=main + prev + nxt + [
            pl.BlockSpec((None, C_CONV, C_CONV_CH), lambda i: (layer, 0, 0)),
            pl.BlockSpec((None, 1, C_CONV_CH), lambda i: (layer, 0, 0)),
        ],
        out_specs=pl.BlockSpec((rows, C_CONV_CH), lambda i: (i, 0)),
        out_shape=jax.ShapeDtypeStruct((t, C_CONV_CH), F32),
        compiler_params=_cparams("parallel"),
        name="conv",
    )(*([p] * (3 * ncol)), conv_w, conv_b)


def _ssd_kernel(xf_ref, xb_ref, dtf_ref, dtb_ref,
                dtbias_ref, alog_ref, dskip_ref, e3_ref, h0f_ref, h0b_ref,
                yf_ref, yb_ref, hfo_ref, hbo_ref, hf_sc, hb_sc,
                *, n_ctx_chunks, ctx_cps, lat_cps):
    s = pl.program_id(0)
    is_ctx = s < n_ctx_chunks
    pos = jnp.where(is_ctx, s % ctx_cps, (s - n_ctx_chunks) % lat_cps)
    cps = jnp.where(is_ctx, ctx_cps, lat_cps)
    first = pos == 0
    last = pos == cps - 1

    @pl.when(first & is_ctx)
    def _():
        hf_sc[...] = jnp.zeros_like(hf_sc)
        hb_sc[...] = jnp.zeros_like(hb_sc)

    @pl.when(first & jnp.logical_not(is_ctx))
    def _():
        hf_sc[...] = h0f_ref[...].T
        hb_sc[...] = h0b_ref[...].T

    def softplus(v):
        return jnp.maximum(v, 0.0) + jnp.log(1.0 + jnp.exp(-jnp.abs(v)))

    a_row = -jnp.exp(alog_ref[...])
    e3 = e3_ref[...]

    dt = softplus(dtf_ref[...] + dtbias_ref[...])
    yf_ref[...] = _ssd_direction(xf_ref[...], dt, a_row, e3[0], hf_sc, 0, True, dskip_ref[...])

    dt = softplus(dtb_ref[...] + dtbias_ref[...])
    yb_ref[...] = _ssd_direction(xb_ref[...], dt, a_row, e3[1], hb_sc, C_HEADS, False, None)

    @pl.when(last & is_ctx)
    def _():
        hfo_ref[...] = hf_sc[...].T
        hbo_ref[...] = hb_sc[...].T


def _ssd_call(xact, dt, dtbias, alog, dskip_e, e3, h0f, h0b, layer, dims):
    batch, seq, dec_batch, dec_seq = dims
    L = C_CHUNK
    ctx_cps, lat_cps = seq // L, dec_seq // L
    n_ctx_chunks = batch * ctx_cps
    n_chunks = n_ctx_chunks + dec_batch * lat_cps
    t = xact.shape[0]

    def mirror(s):
        c_ctx = (s // ctx_cps) * ctx_cps + (ctx_cps - 1 - s % ctx_cps)
        r = s - n_ctx_chunks
        c_lat = n_ctx_chunks + (r // lat_cps) * lat_cps + (lat_cps - 1 - r % lat_cps)
        return jnp.where(s < n_ctx_chunks, c_ctx, c_lat)

    def lat_b(s):
        return jnp.maximum(s - n_ctx_chunks, 0) // lat_cps

    def ctx_b(s):
        return jnp.minimum(s // ctx_cps, batch - 1)

    const2 = lambda s: (0, 0)
    hspec = pl.BlockSpec((None, None, C_INNER, C_STATE), lambda s: (lat_b(s), layer, 0, 0))
    ospec = pl.BlockSpec((None, C_INNER, C_STATE), lambda s: (ctx_b(s), 0, 0))
    return pl.pallas_call(
        functools.partial(_ssd_kernel, n_ctx_chunks=n_ctx_chunks, ctx_cps=ctx_cps, lat_cps=lat_cps),
        grid=(n_chunks,),
        in_specs=[
            pl.BlockSpec((L, C_CONV_CH), lambda s: (s, 0)),
            pl.BlockSpec((L, C_CONV_CH), lambda s: (mirror(s), 0)),
            pl.BlockSpec((L, LANES), lambda s: (s, 0)),
            pl.BlockSpec((L, LANES), lambda s: (mirror(s), 0)),
            pl.BlockSpec((1, LANES), const2),
            pl.BlockSpec((1, LANES), const2),
            pl.BlockSpec((1, C_INNER), const2),
            pl.BlockSpec((2, 3 * LANES, C_INNER), lambda s: (0, 0, 0)),
            hspec, hspec,
        ],
        out_specs=[
            pl.BlockSpec((L, C_INNER), lambda s: (s, 0)),
            pl.BlockSpec((L, C_INNER), lambda s: (mirror(s), 0)),
            ospec, ospec,
        ],
        out_shape=[
            jax.ShapeDtypeStruct((t, C_INNER), F32), jax.ShapeDtypeStruct((t, C_INNER), F32),
            jax.ShapeDtypeStruct((batch, C_INNER, C_STATE), F32),
            jax.ShapeDtypeStruct((batch, C_INNER, C_STATE), F32),
        ],
        scratch_shapes=[
            pltpu.VMEM((C_STATE, C_INNER), F32),
            pltpu.VMEM((C_STATE, C_INNER), F32),
        ],
        compiler_params=_cparams("arbitrary"),
        name="ssd",
    )(xact, xact, dt, dt, dtbias, alog, dskip_e, e3, h0f, h0b)


def _outproj_kernel(oa_ref, od_ref, yf_ref, yb_ref, z0_ref, z1_ref, sn_ref, h_ref, mod_ref, w_ref, o_ref,
                    mix_sc):
    j = pl.program_id(1)
    tn = o_ref.shape[1]

    @pl.when(j == 0)
    def _():
        na = oa_ref.shape[1]
        nd = od_ref.shape[1]
        mix_sc[:, 0:na] = oa_ref[...]
        mix_sc[:, na:na + nd] = od_ref[...]
        gw = HALF_INNER
        for g, z_ref in enumerate((z0_ref, z1_ref)):
            z = z_ref[...]
            yg = (yf_ref[:, g * gw:(g + 1) * gw] + yb_ref[:, g * gw:(g + 1) * gw]) * (z * _sigmoid(z))
            yg = yg * lax.rsqrt(jnp.mean(yg * yg, axis=-1, keepdims=True) + EPS)
            yg = yg * sn_ref[:, g * gw:(g + 1) * gw]
            mix_sc[:, na + nd + g * gw:na + nd + (g + 1) * gw] = yg.astype(BF16)

    gate = mod_ref[2:3, pl.ds(pl.multiple_of(j * tn, tn), tn)]
    o_ref[...] = h_ref[...] + gate * _dot(mix_sc[...], w_ref[...])


def _outproj_call(oa, od, yf, yb, p, ssm_norm, h, mod_l, w, layer, mod_row, tm, tn=1024):
    t, d = h.shape
    tn = min(tn, d)
    mw = w.shape[1]
    z0 = COL_Z * LANES // HALF_INNER
    return pl.pallas_call(
        _outproj_kernel,
        grid=(t // tm, d // tn),
        in_specs=[
            pl.BlockSpec((tm, oa.shape[1]), lambda i, j: (i, 0)),
            pl.BlockSpec((tm, od.shape[1]), lambda i, j: (i, 0)),
            pl.BlockSpec((tm, C_INNER), lambda i, j: (i, 0)),
            pl.BlockSpec((tm, C_INNER), lambda i, j: (i, 0)),
            pl.BlockSpec((tm, HALF_INNER), lambda i, j: (i, z0)),
            pl.BlockSpec((tm, HALF_INNER), lambda i, j: (i, z0 + 1)),
            pl.BlockSpec((1, C_INNER), lambda i, j: (0, 0)),
            pl.BlockSpec((tm, tn), lambda i, j: (i, j)),
            pl.BlockSpec((None, N_MOD, d), lambda i, j: (mod_row(i * tm), 0, 0)),
            pl.BlockSpec((None, mw, tn), lambda i, j: (layer, 0, j)),
        ],
        out_specs=pl.BlockSpec((tm, tn), lambda i, j: (i, j)),
        out_shape=jax.ShapeDtypeStruct((t, d), F32),
        scratch_shapes=[pltpu.VMEM((tm, mw), BF16)],
        compiler_params=_cparams("parallel", "arbitrary"),
        name="outproj",
    )(oa, od, yf, yb, p, p, ssm_norm, h, mod_l, w)


def _ffn_kernel(h_ref, mod_ref, nw_ref, wg_ref, wu_ref, wd_ref, o_ref, u_sc):
    f = pl.program_id(1)

    @pl.when(f == 0)
    def _():
        u_sc[...] = _modnorm(h_ref[...], nw_ref[...], mod_ref[3:4, :], mod_ref[4:5, :]).astype(BF16)
        o_ref[...] = jnp.zeros_like(o_ref)

    u = u_sc[...]
    g = _dot(u, wg_ref[...])
    a = (g * _sigmoid(g)) * _dot(u, wu_ref[...])
    o_ref[...] += _dot(a.astype(BF16), wd_ref[...])

    @pl.when(f == pl.num_programs(1) - 1)
    def _():
        o_ref[...] = h_ref[...] + mod_ref[5:6, :] * o_ref[...]


def _ffn_call(h, mod_l, nw, w_gu, w_d, layer, mod_row, tm, tf):
    t, d = h.shape
    ff = w_d.shape[1]
    nf = ff // tf
    return pl.pallas_call(
        _ffn_kernel,
        grid=(t // tm, nf),
        in_specs=[
            pl.BlockSpec((tm, d), lambda i, f: (i, 0)),
            pl.BlockSpec((None, N_MOD, d), lambda i, f: (mod_row(i * tm), 0, 0)),
            pl.BlockSpec((1, d), lambda i, f: (0, 0)),
            pl.BlockSpec((None, d, tf), lambda i, f: (layer, 0, f)),
            pl.BlockSpec((None, d, tf), lambda i, f: (layer, 0, nf + f)),
            pl.BlockSpec((None, tf, d), lambda i, f: (layer, f, 0)),
        ],
        out_specs=pl.BlockSpec((tm, d), lambda i, f: (i, 0)),
        out_shape=jax.ShapeDtypeStruct((t, d), F32),
        scratch_shapes=[pltpu.VMEM((tm, d), BF16)],
        compiler_params=_cparams("parallel", "arbitrary"),
        name="ffn",
    )(h, mod_l, nw, w_gu, w_gu, w_d)


def _final_norm_kernel(h_ref, w_ref, oc_ref, ol_ref, *, n_ctx_tiles):
    x = h_ref[...]
    y = x * lax.rsqrt(jnp.mean(x * x, axis=-1, keepdims=True) + EPS) * w_ref[...]
    i = pl.program_id(0)

    @pl.when(i < n_ctx_tiles)
    def _():
        oc_ref[...] = y

    @pl.when(i >= n_ctx_tiles)
    def _():
        ol_ref[...] = y


def _final_norm_call(h, w, n_ctx, tm):
    t, d = h.shape
    nc = n_ctx // tm
    return pl.pallas_call(
        functools.partial(_final_norm_kernel, n_ctx_tiles=nc),
        grid=(t // tm,),
        in_specs=[pl.BlockSpec((tm, d), lambda i: (i, 0)), pl.BlockSpec((1, d), lambda i: (0, 0))],
        out_specs=[pl.BlockSpec((tm, d), lambda i: (jnp.minimum(i, nc - 1), 0)),
                   pl.BlockSpec((tm, d), lambda i: (jnp.maximum(i - nc, 0), 0))],
        out_shape=[jax.ShapeDtypeStruct((n_ctx, d), F32), jax.ShapeDtypeStruct((t - n_ctx, d), F32)],
        compiler_params=_cparams("arbitrary"),
        name="final_norm",
    )(h, w)


def _rope_tables(n, rot_dim):
    rows = n // GRID_W
    row = jnp.repeat(jnp.arange(rows), GRID_W).astype(F32)
    col = (jnp.arange(rows * GRID_W) % GRID_W).astype(F32)
    quarter = rot_dim // 4
    inv = ROPE_BASE ** (-jnp.arange(quarter, dtype=F32) / quarter)
    ang = jnp.concatenate([row[:, None] * inv, col[:, None] * inv], axis=-1)
    c, s = jnp.cos(ang), jnp.sin(ang)
    reps = LANES // rot_dim
    return (jnp.tile(jnp.concatenate([c, c], axis=-1), (1, reps)),
            jnp.tile(jnp.concatenate([-s, s], axis=-1), (1, reps)))


def _lambda_init(layer):
    return 0.8 - 0.6 * math.exp(-0.3 * layer)


def _pad_lanes(v):
    return jnp.pad(v.reshape(v.shape[0], 1, -1), ((0, 0), (0, 0), (0, LANES - DT_COLS)))


def kernel(x_prompt, x_sample, cache_attn_k, cache_attn_v, cache_diff_k, cache_diff_v, state_ssm_fwd, state_ssm_bwd, c, c_ctx, w_ada, b_ada, norm_mix, norm_ffn, w_in, attn_sink, diff_lambda, diff_norm, conv_w, conv_b, dt_bias, a_log, d_skip, ssm_norm, w_out, w_gate_up, w_down, norm_final):
    batch, seq, d = x_prompt.shape
    dec_batch, dec_seq, _ = x_sample.shape
    depth = w_in.shape[0]
    past = cache_attn_k.shape[2]
    n_ctx = batch * seq
    dims = (batch, seq, dec_batch, dec_seq)
    tm = 512 if (n_ctx % 512 == 0 and dec_seq % 512 == 0) else 256
    tm_big = 1024 if (n_ctx % 1024 == 0 and dec_seq % 1024 == 0) else tm
    tq = min(256, dec_seq)
    assert n_ctx % dec_seq == 0 and n_ctx % tm == 0 and dec_seq % tm == 0
    assert seq % C_CHUNK == 0 and dec_seq % C_CHUNK == 0 and dec_seq % GRID_W == 0
    assert 1 + dec_batch <= 8

    def mod_row(start):
        return jnp.where(start < n_ctx, 0, 1 + (start - n_ctx) // dec_seq)

    w_in_b = w_in.astype(BF16)
    w_in_dt = jnp.pad(w_in_b[:, :, P_COLS:], ((0, 0), (0, 0), (0, LANES - DT_COLS)))
    w_out_b = w_out.astype(BF16)
    w_gu_b = w_gate_up.astype(BF16)
    w_down_b = w_down.astype(BF16)
    cos_a, sin_a = _rope_tables(dec_seq, HEAD_DIM)
    cos_d, sin_d = _rope_tables(dec_seq, B_HALF)
    conv_b3 = conv_b.reshape(depth, 1, C_CONV_CH)
    dtbias = _pad_lanes(dt_bias)
    alog = _pad_lanes(a_log)
    dskip_e = jnp.repeat(d_skip, C_HEAD_DIM, axis=-1).reshape(depth, 1, C_INNER)
    head_of_lane = jnp.arange(C_INNER) // C_HEAD_DIM
    e_f = (jnp.arange(LANES)[:, None] == head_of_lane[None, :]).astype(BF16)
    e_b = (jnp.arange(LANES)[:, None] == head_of_lane[None, :] + C_HEADS).astype(BF16)
    e3 = jnp.stack([jnp.concatenate([e_f] * 3, axis=0), jnp.concatenate([e_b] * 3, axis=0)])
    ck_a = cache_attn_k.reshape(dec_batch, depth, past, A_KV_HEADS * HEAD_DIM)
    cv_a = cache_attn_v.reshape(dec_batch, depth, past, A_KV_HEADS * HEAD_DIM)
    ck_d = cache_diff_k.reshape(dec_batch, depth, past, B_HEADS * HEAD_DIM)
    cv_d = cache_diff_v.reshape(dec_batch, depth, past, B_HEADS * HEAD_DIM)
    h0f = state_ssm_fwd.reshape(dec_batch, depth, C_INNER, C_STATE)
    h0b = state_ssm_bwd.reshape(dec_batch, depth, C_INNER, C_STATE)

    cond8 = jnp.concatenate([c_ctx[None, :], c, jnp.zeros((8 - 1 - dec_batch, d), F32)], axis=0)
    mod = _ada_call(cond8, w_ada, b_ada)[:, :1 + dec_batch].reshape(depth, 1 + dec_batch, N_MOD, d)

    h = jnp.concatenate([x_prompt.reshape(n_ctx, d), x_sample.reshape(dec_batch * dec_seq, d)], axis=0)
    ctx_out = []
    for l in range(depth):
        lam_init = _lambda_init(l)
        p, dt = _inproj_call(h, mod[l], norm_mix[l][None, :], w_in_b, w_in_dt, l, mod_row, tm_big)
        oa = _attn_a_calls(p, attn_sink, ck_a, cv_a, cos_a, sin_a, l, dims)
        od = _attn_b_calls(p, diff_lambda[l], diff_norm[l][None, :], ck_d, cv_d, cos_d, sin_d,
                           l, lam_init, dims, tq)
        xact = _conv_call(p, conv_w, conv_b3, l, dims)
        yf, yb, hf, hb = _ssd_call(xact, dt, dtbias[l], alog[l], dskip_e[l], e3, h0f, h0b, l, dims)
        h = _outproj_call(oa, od, yf, yb, p, ssm_norm[l][None, :], h, mod[l], w_out_b, l, mod_row, tm)
        h = _ffn_call(h, mod[l], norm_ffn[l][None, :], w_gu_b, w_down_b, l, mod_row, tm, 512)

        def ctx_cols(lo, hi, heads):
            return p[:n_ctx, lo * LANES:hi * LANES].reshape(batch, seq, heads, HEAD_DIM)

        ctx_out.append((ctx_cols(COL_KA, COL_VA, A_KV_HEADS), ctx_cols(COL_VA, COL_QD, A_KV_HEADS),
                        ctx_cols(COL_KD, COL_VD, B_HEADS), ctx_cols(COL_VD, COL_Z, B_HEADS),
                        hf.reshape(batch, C_HEADS, C_HEAD_DIM, C_STATE),
                        hb.reshape(batch, C_HEADS, C_HEAD_DIM, C_STATE)))

    y_prompt, y_sample = _final_norm_call(h, norm_final[None, :], n_ctx, tm)
    stacked = [jnp.stack([t[i] for t in ctx_out], axis=1) for i in range(6)]
    return (y_prompt.reshape(batch, seq, d), y_sample.reshape(dec_batch, dec_seq, d), *stacked)
```

```python
import functools
import math

import jax
import jax.numpy as jnp
from jax import lax
from jax.experimental import pallas as pl
from jax.experimental.pallas import tpu as pltpu

F32 = jnp.float32
BF16 = jnp.bfloat16

HEAD_DIM = 128
A_HEADS = 4
A_KV_HEADS = 2
A_GROUP = A_HEADS // A_KV_HEADS
BLOCK = 128
B_HEADS = 4
B_HALF = HEAD_DIM // 2
C_HEADS = 16
C_HEAD_DIM = 64
C_INNER = C_HEADS * C_HEAD_DIM
C_GROUPS = 2
C_STATE = 128
C_CONV = 5
C_CHUNK = 128
C_CONV_CH = C_INNER + 2 * C_GROUPS * C_STATE
GRID_W = 64
EPS = 1e-6
ROPE_BASE = 10000.0
N_MOD = 6
LANES = 128
HALO = 8
NEG = -1e30
LOG2E = math.log2(math.e)

COL_QA, COL_KA, COL_VA, COL_QD, COL_KD, COL_VD, COL_Z, COL_XBC = 0, 4, 6, 8, 12, 16, 20, 28
P_COLS = 40 * LANES
HALF_INNER = C_INNER // C_GROUPS
DT_COLS = 2 * C_HEADS

VMEM_LIMIT = 48 * 1024 * 1024


def _cparams(*sem):
    return pltpu.CompilerParams(dimension_semantics=sem, vmem_limit_bytes=VMEM_LIMIT)


def _dot(a, b):
    return jnp.dot(a, b, preferred_element_type=F32)


def _dot_nt(a, b):
    return lax.dot_general(a, b, (((1,), (1,)), ((), ())), preferred_element_type=F32)


def _sigmoid(x):
    return 1.0 / (1.0 + jnp.exp(-x))


def _split3(a):
    hi = a.astype(BF16)
    r = a - hi.astype(F32)
    mid = r.astype(BF16)
    lo = (r - mid.astype(F32)).astype(BF16)
    return hi, mid, lo


def _ada_kernel(cond_ref, w_ref, b_ref, o_ref):
    s = cond_ref[...]
    s = s * _sigmoid(s)
    o_ref[...] = _dot(s.astype(BF16), w_ref[...].astype(BF16)) + b_ref[...]


def _ada_call(cond8, w_ada, b_ada, tn=1024):
    depth, d, n = w_ada.shape
    return pl.pallas_call(
        _ada_kernel,
        grid=(depth, n // tn),
        in_specs=[
            pl.BlockSpec((8, d), lambda l, j: (0, 0)),
            pl.BlockSpec((None, d, tn), lambda l, j: (l, 0, j)),
            pl.BlockSpec((None, 1, tn), lambda l, j: (l, 0, j)),
        ],
        out_specs=pl.BlockSpec((None, 8, tn), lambda l, j: (l, 0, j)),
        out_shape=jax.ShapeDtypeStruct((depth, 8, n), F32),
        compiler_params=_cparams("parallel", "parallel"),
        name="adaln",
    )(cond8, w_ada, b_ada.reshape(depth, 1, n))


def _modnorm(x, nw, shift, scale):
    ms = jnp.mean(x * x, axis=-1, keepdims=True)
    y = x * lax.rsqrt(ms + EPS) * nw
    return y * (1.0 + scale) + shift


def _inproj_kernel(h_ref, mod_ref, nw_ref, w_ref, wdt_ref, p_ref, dt_ref, u_sc):
    @pl.when(pl.program_id(1) == 0)
    def _():
        u = _modnorm(h_ref[...], nw_ref[...], mod_ref[0:1, :], mod_ref[1:2, :]).astype(BF16)
        u_sc[...] = u
        dt_ref[...] = _dot(u, wdt_ref[...])

    p_ref[...] = _dot(u_sc[...], w_ref[...])


def _inproj_call(h, mod_l, nw, w, wdt, layer, mod_row, tm, tn=1024):
    t, d = h.shape
    return pl.pallas_call(
        _inproj_kernel,
        grid=(t // tm, P_COLS // tn),
        in_specs=[
            pl.BlockSpec((tm, d), lambda i, j: (i, 0)),
            pl.BlockSpec((None, N_MOD, d), lambda i, j: (mod_row(i * tm), 0, 0)),
            pl.BlockSpec((1, d), lambda i, j: (0, 0)),
            pl.BlockSpec((None, d, tn), lambda i, j: (layer, 0, j)),
            pl.BlockSpec((None, d, LANES), lambda i, j: (layer, 0, 0)),
        ],
        out_specs=[
            pl.BlockSpec((tm, tn), lambda i, j: (i, j)),
            pl.BlockSpec((tm, LANES), lambda i, j: (i, 0)),
        ],
        out_shape=[jax.ShapeDtypeStruct((t, P_COLS), F32), jax.ShapeDtypeStruct((t, LANES), F32)],
        scratch_shapes=[pltpu.VMEM((tm, d), BF16)],
        compiler_params=_cparams("parallel", "arbitrary"),
        name="inproj",
    )(h, mod_l, nw, w, wdt)


def _rope_a(x, c, s):
    return x * c + pltpu.roll(x, HEAD_DIM // 2, axis=1) * s


def _rope_d(x, c, s):
    lane = lax.broadcasted_iota(jnp.int32, x.shape, 1)
    q = B_HALF // 2
    partner = jnp.where((lane & (B_HALF - 1)) < q, pltpu.roll(x, LANES - q, axis=1), pltpu.roll(x, q, axis=1))
    return x * c + partner * s


def _sink_softmax_pv(s, sink, v):
    m = jnp.maximum(jnp.max(s, axis=-1, keepdims=True), sink)
    e = jnp.exp2(s - m)
    den = jnp.sum(e, axis=-1, keepdims=True) + jnp.exp2(sink - m)
    return _dot(e.astype(BF16), v) / den


def _attn_a_ctx_kernel(sink_ref, q_ref, k_ref, v_ref, buf_ref, o_ref, *, layer):
    del buf_ref
    scale = HEAD_DIM ** -0.5 * LOG2E
    for kv in range(A_KV_HEADS):
        ks = slice(kv * HEAD_DIM, (kv + 1) * HEAD_DIM)
        k = k_ref[:, ks].astype(BF16)
        v = v_ref[:, ks].astype(BF16)
        for g in range(A_GROUP):
            head = kv * A_GROUP + g
            sl = slice(head * HEAD_DIM, (head + 1) * HEAD_DIM)
            q = (q_ref[:, sl] * scale).astype(BF16)
            o = _sink_softmax_pv(_dot_nt(q, k), sink_ref[layer, head] * LOG2E, v)
            o_ref[:, sl] = o.astype(o_ref.dtype)


def _attn_a_lat_kernel(sink_ref, q_ref, kp_ref, kc_ref, kn_ref, vp_ref, vc_ref, vn_ref,
                       kx_ref, vx_ref, cq_ref, sq_ref, cp_ref, sp_ref, cn_ref, sn_ref,
                       ctx_rows_ref, o_ref, *, layer, n_blocks):
    del ctx_rows_ref
    nb = pl.program_id(1)
    scale = HEAD_DIM ** -0.5 * LOG2E
    cq, sq = cq_ref[...], sq_ref[...]
    cp, sp, cn, sn = cp_ref[...], sp_ref[...], cn_ref[...], sn_ref[...]
    n_ctx = kx_ref.shape[0]
    qi = lax.broadcasted_iota(jnp.int32, (BLOCK, 3 * BLOCK + n_ctx), 0)
    kj = lax.broadcasted_iota(jnp.int32, (BLOCK, 3 * BLOCK + n_ctx), 1)
    bad_prev = (kj < BLOCK) & ((kj < qi) | (nb == 0))
    bad_next = (kj >= 2 * BLOCK) & (kj < 3 * BLOCK) & ((kj - 2 * BLOCK > qi) | (nb == n_blocks - 1))
    mask = jnp.logical_not(bad_prev | bad_next)
    for kv in range(A_KV_HEADS):
        ks = slice(kv * HEAD_DIM, (kv + 1) * HEAD_DIM)
        k_all = jnp.concatenate([
            _rope_a(kp_ref[:, ks], cp, sp).astype(BF16),
            _rope_a(kc_ref[:, ks], cq, sq).astype(BF16),
            _rope_a(kn_ref[:, ks], cn, sn).astype(BF16),
            kx_ref[:, ks].astype(BF16)], axis=0)
        v_all = jnp.concatenate([vp_ref[:, ks], vc_ref[:, ks], vn_ref[:, ks], vx_ref[:, ks]],
                                axis=0).astype(BF16)
        for g in range(A_GROUP):
            head = kv * A_GROUP + g
            sl = slice(head * HEAD_DIM, (head + 1) * HEAD_DIM)
            q = (_rope_a(q_ref[:, sl], cq, sq) * scale).astype(BF16)
            s = jnp.where(mask, _dot_nt(q, k_all), NEG)
            o = _sink_softmax_pv(s, sink_ref[layer, head] * LOG2E, v_all)
            o_ref[:, sl] = o.astype(o_ref.dtype)


def _attn_a_calls(p, sink, ck, cv, cos_a, sin_a, layer, dims):
    t = p.shape[0]
    batch, seq, dec_batch, dec_seq = dims
    n_ctx = batch * seq
    smem = pl.BlockSpec(memory_space=pltpu.SMEM)
    aw = A_HEADS * HEAD_DIM
    kw = A_KV_HEADS * HEAD_DIM
    qcol, kcol, vcol = COL_QA * LANES // aw, COL_KA * LANES // kw, COL_VA * LANES // kw
    oa_ctx = pl.pallas_call(
        functools.partial(_attn_a_ctx_kernel, layer=layer),
        grid=(batch,),
        in_specs=[
            smem,
            pl.BlockSpec((seq, aw), lambda b: (b, qcol)),
            pl.BlockSpec((seq, kw), lambda b: (b, kcol)),
            pl.BlockSpec((seq, kw), lambda b: (b, vcol)),
            pl.BlockSpec(memory_space=pl.ANY),
        ],
        out_specs=pl.BlockSpec((seq, aw), lambda b: (b, 0)),
        out_shape=jax.ShapeDtypeStruct((t, aw), BF16),
        input_output_aliases={4: 0},
        compiler_params=_cparams("parallel"),
        name="attn_a_ctx",
    )(sink, p, p, p, jnp.zeros((t, aw), BF16))

    nbl = dec_seq // BLOCK
    base = n_ctx // BLOCK

    def row(b, nb):
        return base + b * nbl + nb

    def prev(nb):
        return jnp.maximum(nb - 1, 0)

    def nxt(nb):
        return jnp.minimum(nb + 1, nbl - 1)

    past = ck.shape[2]
    kblk = (BLOCK, kw)
    tblk = (BLOCK, HEAD_DIM)
    oa_lat = pl.pallas_call(
        functools.partial(_attn_a_lat_kernel, layer=layer, n_blocks=nbl),
        grid=(dec_batch, nbl),
        in_specs=[
            smem,
            pl.BlockSpec((BLOCK, aw), lambda b, n: (row(b, n), qcol)),
            pl.BlockSpec(kblk, lambda b, n: (row(b, prev(n)), kcol)),
            pl.BlockSpec(kblk, lambda b, n: (row(b, n), kcol)),
            pl.BlockSpec(kblk, lambda b, n: (row(b, nxt(n)), kcol)),
            pl.BlockSpec(kblk, lambda b, n: (row(b, prev(n)), vcol)),
            pl.BlockSpec(kblk, lambda b, n: (row(b, n), vcol)),
            pl.BlockSpec(kblk, lambda b, n: (row(b, nxt(n)), vcol)),
            pl.BlockSpec((None, None, past, kw), lambda b, n: (b, layer, 0, 0)),
            pl.BlockSpec((None, None, past, kw), lambda b, n: (b, layer, 0, 0)),
            pl.BlockSpec(tblk, lambda b, n: (n, 0)),
            pl.BlockSpec(tblk, lambda b, n: (n, 0)),
            pl.BlockSpec(tblk, lambda b, n: (prev(n), 0)),
            pl.BlockSpec(tblk, lambda b, n: (prev(n), 0)),
            pl.BlockSpec(tblk, lambda b, n: (nxt(n), 0)),
            pl.BlockSpec(tblk, lambda b, n: (nxt(n), 0)),
            pl.BlockSpec(memory_space=pl.ANY),
        ],
        out_specs=pl.BlockSpec((BLOCK, aw), lambda b, n: (row(b, n), 0)),
        out_shape=jax.ShapeDtypeStruct((t, aw), BF16),
        input_output_aliases={16: 0},
        compiler_params=_cparams("parallel", "parallel"),
        name="attn_a_lat",
    )(sink, p, p, p, p, p, p, p, ck, cv, cos_a, sin_a, cos_a, sin_a, cos_a, sin_a, oa_ctx)
    return oa_lat


def _diff_lambda(lv, lam_init):
    a = jnp.sum(lv[0:1, :] * lv[1:2, :], axis=-1, keepdims=True)
    b = jnp.sum(lv[2:3, :] * lv[3:4, :], axis=-1, keepdims=True)
    return jnp.exp(a) - jnp.exp(b) + lam_init


def _diff_core(q, k, v, lam, dn_w, lam_init):
    lane = lax.broadcasted_iota(jnp.int32, q.shape, 1)
    qs = q * (B_HALF ** -0.5 * LOG2E)
    outs = []
    for half in range(2):
        sel = (lane < B_HALF) if half == 0 else (lane >= B_HALF)
        s = _dot_nt(jnp.where(sel, qs, 0.0).astype(BF16), k)
        e = jnp.exp2(s - jnp.max(s, axis=-1, keepdims=True))
        den = jnp.sum(e, axis=-1, keepdims=True)
        outs.append(_dot(e.astype(BF16), v) / den)
    o = outs[0] - lam * outs[1]
    ms = jnp.mean(o * o, axis=-1, keepdims=True)
    return o * lax.rsqrt(ms + EPS) * dn_w * (1.0 - lam_init)


def _attn_b_ctx_kernel(lv_ref, dn_ref, q_ref, k_ref, v_ref, buf_ref, o_ref, *, lam_init):
    del buf_ref
    lam = _diff_lambda(lv_ref[...], lam_init)
    for head in range(B_HEADS):
        sl = slice(head * HEAD_DIM, (head + 1) * HEAD_DIM)
        o = _diff_core(q_ref[:, sl], k_ref[:, sl].astype(BF16), v_ref[:, sl].astype(BF16), lam, dn_ref[...],
                       lam_init)
        o_ref[:, sl] = o.astype(o_ref.dtype)


def _attn_b_lat_kernel(lv_ref, dn_ref, q_ref, k_ref, v_ref, kx_ref, vx_ref, cq_ref, sq_ref,
                       ck_ref, sk_ref, ctx_rows_ref, o_ref, k_sc, v_sc, *, lam_init):
    del ctx_rows_ref
    n = k_ref.shape[0]

    @pl.when(pl.program_id(2) == 0)
    def _():
        k_sc[0:n, :] = _rope_d(k_ref[...], ck_ref[...], sk_ref[...]).astype(BF16)
        k_sc[n:, :] = kx_ref[...].astype(BF16)
        v_sc[0:n, :] = v_ref[...].astype(BF16)
        v_sc[n:, :] = vx_ref[...].astype(BF16)

    lam = _diff_lambda(lv_ref[...], lam_init)
    q = _rope_d(q_ref[...], cq_ref[...], sq_ref[...])
    o = _diff_core(q, k_sc[...], v_sc[...], lam, dn_ref[...], lam_init)
    o_ref[...] = o.astype(o_ref.dtype)


def _attn_b_calls(p, lv, dn_w, ck, cv, cos_d, sin_d, layer, lam_init, dims, tq):
    batch, seq, dec_batch, dec_seq = dims
    n_ctx = batch * seq
    hd = HEAD_DIM
    bw = B_HEADS * hd
    od_ctx = pl.pallas_call(
        functools.partial(_attn_b_ctx_kernel, lam_init=lam_init),
        grid=(batch,),
        in_specs=[
            pl.BlockSpec((4, B_HALF), lambda b: (0, 0)),
            pl.BlockSpec((1, hd), lambda b: (0, 0)),
            pl.BlockSpec((seq, bw), lambda b: (b, COL_QD * LANES // bw)),
            pl.BlockSpec((seq, bw), lambda b: (b, COL_KD * LANES // bw)),
            pl.BlockSpec((seq, bw), lambda b: (b, COL_VD * LANES // bw)),
            pl.BlockSpec(memory_space=pl.ANY),
        ],
        out_specs=pl.BlockSpec((seq, bw), lambda b: (b, 0)),
        out_shape=jax.ShapeDtypeStruct((p.shape[0], bw), BF16),
        input_output_aliases={5: 0},
        compiler_params=_cparams("parallel"),
        name="attn_b_ctx",
    )(lv, dn_w, p, p, p, jnp.zeros((p.shape[0], bw), BF16))

    past = ck.shape[2]
    nq = dec_seq // tq
    qbase = n_ctx // tq
    sbase = n_ctx // dec_seq
    od_lat = pl.pallas_call(
        functools.partial(_attn_b_lat_kernel, lam_init=lam_init),
        grid=(dec_batch, B_HEADS, nq),
        in_specs=[
            pl.BlockSpec((4, B_HALF), lambda b, h, i: (0, 0)),
            pl.BlockSpec((1, hd), lambda b, h, i: (0, 0)),
            pl.BlockSpec((tq, hd), lambda b, h, i: (qbase + b * nq + i, COL_QD + h)),
            pl.BlockSpec((dec_seq, hd), lambda b, h, i: (sbase + b, COL_KD + h)),
            pl.BlockSpec((dec_seq, hd), lambda b, h, i: (sbase + b, COL_VD + h)),
            pl.BlockSpec((None, None, past, hd), lambda b, h, i: (b, layer, 0, h)),
            pl.BlockSpec((None, None, past, hd), lambda b, h, i: (b, layer, 0, h)),
            pl.BlockSpec((tq, hd), lambda b, h, i: (i, 0)),
            pl.BlockSpec((tq, hd), lambda b, h, i: (i, 0)),
            pl.BlockSpec((dec_seq, hd), lambda b, h, i: (0, 0)),
            pl.BlockSpec((dec_seq, hd), lambda b, h, i: (0, 0)),
            pl.BlockSpec(memory_space=pl.ANY),
        ],
        out_specs=pl.BlockSpec((tq, hd), lambda b, h, i: (qbase + b * nq + i, h)),
        out_shape=jax.ShapeDtypeStruct((p.shape[0], B_HEADS * hd), BF16),
        input_output_aliases={11: 0},
        scratch_shapes=[pltpu.VMEM((dec_seq + past, hd), BF16), pltpu.VMEM((dec_seq + past, hd), BF16)],
        compiler_params=_cparams("parallel", "parallel", "arbitrary"),
        name="attn_b_lat",
    )(lv, dn_w, p, p, p, ck, cv, cos_d, sin_d, cos_d, sin_d, od_ctx)
    return od_lat


def _expand_heads(a, e3):
    return _dot(jnp.concatenate(_split3(a), axis=1), e3)


def _ssd_direction(xact, dt, a_row, e3, h_sc, hoff, fwd, dskip):
    L = C_CHUNK
    li = lax.broadcasted_iota(jnp.int32, (L, L), 0)
    si = lax.broadcasted_iota(jnp.int32, (L, L), 1)
    tri = (li >= si) if fwd else (li <= si)
    tri_b = jnp.where(tri, 1.0, 0.0).astype(BF16)
    dta = dt * a_row
    cum = _dot(jnp.concatenate([tri_b, tri_b, tri_b], axis=1), jnp.concatenate(_split3(dta), axis=0))
    cum_t = cum.T
    dt_t = dt.T
    end = cum[L - 1:L, :] if fwd else cum[0:1, :]
    to_end = jnp.exp(end - cum) * dt
    ecum = jnp.exp(cum)
    x = xact[:, :C_INNER]
    x_te = (x * _expand_heads(to_end, e3)).astype(BF16)
    dec_row = _expand_heads(jnp.broadcast_to(jnp.exp(end), (8, LANES)), e3)[0:1, :]
    lane = lax.broadcasted_iota(jnp.int32, (L, LANES), 1)
    lo = lane < C_HEAD_DIM
    gw = C_INNER // C_GROUPS
    ys = []
    for g in range(C_GROUPS):
        bm = xact[:, C_INNER + g * C_STATE:C_INNER + (g + 1) * C_STATE]
        cm = xact[:, C_INNER + (C_GROUPS + g) * C_STATE:C_INNER + (C_GROUPS + g + 1) * C_STATE]
        cb = _dot_nt(cm.astype(BF16), bm.astype(BF16))
        h_prev = h_sc[:, g * gw:(g + 1) * gw]
        for pair in range(gw // LANES):
            blk = g * (gw // LANES) + pair
            w_parts, c_parts = [], []
            for j in range(2):
                c = hoff + 2 * blk + j
                seg = cum[:, c:c + 1] - cum_t[c:c + 1, :]
                w = cb * jnp.exp(jnp.where(tri, seg, NEG)) * dt_t[c:c + 1, :]
                w_parts.append(w.astype(BF16))
                c_parts.append((cm * ecum[:, c:c + 1]).astype(BF16))
            xb = x[:, blk * LANES:(blk + 1) * LANES]
            hb = h_prev[:, pair * LANES:(pair + 1) * LANES]
            lhs = jnp.concatenate(w_parts + c_parts, axis=1)
            rhs = jnp.concatenate([jnp.where(lo, xb, 0.0), jnp.where(lo, 0.0, xb),
                                   jnp.where(lo, hb, 0.0), jnp.where(lo, 0.0, hb)], axis=0).astype(BF16)
            ys.append(_dot(lhs, rhs))
        st = _dot(bm.T.astype(BF16), x_te[:, g * gw:(g + 1) * gw])
        h_sc[:, g * gw:(g + 1) * gw] = dec_row[:, g * gw:(g + 1) * gw] * h_prev + st
    y = jnp.concatenate(ys, axis=1)
    if dskip is not None:
        y = y + dskip * x
    return y


def _conv_kernel(*refs, ncol, n_ctx_tiles, ctx_tps, lat_tps):
    x_refs, xp_refs, xn_refs = refs[:ncol], refs[ncol:2 * ncol], refs[2 * ncol:3 * ncol]
    cw_ref, cbias_ref, o_ref = refs[3 * ncol:]
    t = pl.program_id(0)
    is_ctx = t < n_ctx_tiles
    pos = jnp.where(is_ctx, t % ctx_tps, (t - n_ctx_tiles) % lat_tps)
    tps = jnp.where(is_ctx, ctx_tps, lat_tps)
    rows = o_ref.shape[0]

    def cols(rs):
        return jnp.concatenate([r[...] for r in rs], axis=1)

    xp = jnp.concatenate([jnp.where(pos > 0, cols(xp_refs), 0.0), cols(x_refs),
                          jnp.where(pos < tps - 1, cols(xn_refs), 0.0)], axis=0)
    n = rows + 2 * HALO
    acc = jnp.broadcast_to(cbias_ref[...], o_ref.shape)
    for k in range(C_CONV):
        shifted = xp if k == C_CONV // 2 else pltpu.roll(xp, (C_CONV // 2 - k) % n, axis=0)
        acc = acc + shifted[HALO:HALO + rows] * cw_ref[k:k + 1, :]
    o_ref[...] = acc * _sigmoid(acc)


def _conv_call(p, conv_w, conv_b, layer, dims, rows=512):
    batch, seq, dec_batch, dec_seq = dims
    t = p.shape[0]
    rows = min(rows, seq)
    assert seq % rows == 0 and dec_seq % rows == 0
    hpt = rows // HALO
    last_halo = t // HALO - 1
    cw = HALF_INNER
    c0 = COL_XBC * LANES // cw
    ncol = C_CONV_CH // cw
    main = [pl.BlockSpec((rows, cw), lambda i, c=c: (i, c0 + c)) for c in range(ncol)]
    prev = [pl.BlockSpec((HALO, cw), lambda i, c=c: (jnp.maximum(i * hpt - 1, 0), c0 + c)) for c in range(ncol)]
    nxt = [pl.BlockSpec((HALO, cw), lambda i, c=c: (jnp.minimum((i + 1) * hpt, last_halo), c0 + c))
           for c in range(ncol)]
    return pl.pallas_call(
        functools.partial(_conv_kernel, ncol=ncol, n_ctx_tiles=batch * seq // rows, ctx_tps=seq // rows,
                          lat_tps=dec_seq // rows),
        grid=(t // rows,),
        in_specs=main + prev + nxt + [
            pl.BlockSpec((None, C_CONV, C_CONV_CH), lambda i: (layer, 0, 0)),
            pl.BlockSpec((None, 1, C_CONV_CH), lambda i: (layer, 0, 0)),
        ],
        out_specs=pl.BlockSpec((rows, C_CONV_CH), lambda i: (i, 0)),
        out_shape=jax.ShapeDtypeStruct((t, C_CONV_CH), F32),
        compiler_params=_cparams("parallel"),
        name="conv",
    )(*([p] * (3 * ncol)), conv_w, conv_b)


def _ssd_kernel(xf_ref, xb_ref, dtf_ref, dtb_ref,
                dtbias_ref, alog_ref, dskip_ref, e3_ref, h0f_ref, h0b_ref,
                yf_ref, yb_ref, hfo_ref, hbo_ref, hf_sc, hb_sc,
                *, n_ctx_chunks, ctx_cps, lat_cps):
    s = pl.program_id(0)
    is_ctx = s < n_ctx_chunks
    pos = jnp.where(is_ctx, s % ctx_cps, (s - n_ctx_chunks) % lat_cps)
    cps = jnp.where(is_ctx, ctx_cps, lat_cps)
    first = pos == 0
    last = pos == cps - 1

    @pl.when(first & is_ctx)
    def _():
        hf_sc[...] = jnp.zeros_like(hf_sc)
        hb_sc[...] = jnp.zeros_like(hb_sc)

    @pl.when(first & jnp.logical_not(is_ctx))
    def _():
        hf_sc[...] = h0f_ref[...].T
        hb_sc[...] = h0b_ref[...].T

    def softplus(v):
        return jnp.maximum(v, 0.0) + jnp.log(1.0 + jnp.exp(-jnp.abs(v)))

    a_row = -jnp.exp(alog_ref[...])
    e3 = e3_ref[...]

    dt = softplus(dtf_ref[...] + dtbias_ref[...])
    yf_ref[...] = _ssd_direction(xf_ref[...], dt, a_row, e3[0], hf_sc, 0, True, dskip_ref[...])

    dt = softplus(dtb_ref[...] + dtbias_ref[...])
    yb_ref[...] = _ssd_direction(xb_ref[...], dt, a_row, e3[1], hb_sc, C_HEADS, False, None)

    @pl.when(last & is_ctx)
    def _():
        hfo_ref[...] = hf_sc[...].T
        hbo_ref[...] = hb_sc[...].T


def _ssd_call(xact, dt, dtbias, alog, dskip_e, e3, h0f, h0b, layer, dims):
    batch, seq, dec_batch, dec_seq = dims
    L = C_CHUNK
    ctx_cps, lat_cps = seq // L, dec_seq // L
    n_ctx_chunks = batch * ctx_cps
    n_chunks = n_ctx_chunks + dec_batch * lat_cps
    t = xact.shape[0]

    def mirror(s):
        c_ctx = (s // ctx_cps) * ctx_cps + (ctx_cps - 1 - s % ctx_cps)
        r = s - n_ctx_chunks
        c_lat = n_ctx_chunks + (r // lat_cps) * lat_cps + (lat_cps - 1 - r % lat_cps)
        return jnp.where(s < n_ctx_chunks, c_ctx, c_lat)

    def lat_b(s):
        return jnp.maximum(s - n_ctx_chunks, 0) // lat_cps

    def ctx_b(s):
        return jnp.minimum(s // ctx_cps, batch - 1)

    const2 = lambda s: (0, 0)
    hspec = pl.BlockSpec((None, None, C_INNER, C_STATE), lambda s: (lat_b(s), layer, 0, 0))
    ospec = pl.BlockSpec((None, C_INNER, C_STATE), lambda s: (ctx_b(s), 0, 0))
    return pl.pallas_call(
        functools.partial(_ssd_kernel, n_ctx_chunks=n_ctx_chunks, ctx_cps=ctx_cps, lat_cps=lat_cps),
        grid=(n_chunks,),
        in_specs=[
            pl.BlockSpec((L, C_CONV_CH), lambda s: (s, 0)),
            pl.BlockSpec((L, C_CONV_CH), lambda s: (mirror(s), 0)),
            pl.BlockSpec((L, LANES), lambda s: (s, 0)),
            pl.BlockSpec((L, LANES), lambda s: (mirror(s), 0)),
            pl.BlockSpec((1, LANES), const2),
            pl.BlockSpec((1, LANES), const2),
            pl.BlockSpec((1, C_INNER), const2),
            pl.BlockSpec((2, 3 * LANES, C_INNER), lambda s: (0, 0, 0)),
            hspec, hspec,
        ],
        out_specs=[
            pl.BlockSpec((L, C_INNER), lambda s: (s, 0)),
            pl.BlockSpec((L, C_INNER), lambda s: (mirror(s), 0)),
            ospec, ospec,
        ],
        out_shape=[
            jax.ShapeDtypeStruct((t, C_INNER), F32), jax.ShapeDtypeStruct((t, C_INNER), F32),
            jax.ShapeDtypeStruct((batch, C_INNER, C_STATE), F32),
            jax.ShapeDtypeStruct((batch, C_INNER, C_STATE), F32),
        ],
        scratch_shapes=[
            pltpu.VMEM((C_STATE, C_INNER), F32),
            pltpu.VMEM((C_STATE, C_INNER), F32),
        ],
        compiler_params=_cparams("arbitrary"),
        name="ssd",
    )(xact, xact, dt, dt, dtbias, alog, dskip_e, e3, h0f, h0b)


def _outproj_kernel(oa_ref, od_ref, yf_ref, yb_ref, z0_ref, z1_ref, sn_ref, h_ref, mod_ref, w_ref, o_ref):
    na = oa_ref.shape[1]
    nd = od_ref.shape[1]
    acc = _dot(oa_ref[...], w_ref[0:na, :]) + _dot(od_ref[...], w_ref[na:na + nd, :])
    gw = HALF_INNER
    for g, z_ref in enumerate((z0_ref, z1_ref)):
        z = z_ref[...]
        yg = (yf_ref[:, g * gw:(g + 1) * gw] + yb_ref[:, g * gw:(g + 1) * gw]) * (z * _sigmoid(z))
        yg = yg * lax.rsqrt(jnp.mean(yg * yg, axis=-1, keepdims=True) + EPS)
        yg = yg * sn_ref[:, g * gw:(g + 1) * gw]
        lo = na + nd + g * gw
        acc = acc + _dot(yg.astype(BF16), w_ref[lo:lo + gw, :])
    o_ref[...] = h_ref[...] + mod_ref[2:3, :] * acc


def _outproj_call(oa, od, yf, yb, p, ssm_norm, h, mod_l, w, layer, mod_row, tm):
    t, d = h.shape
    mw = w.shape[1]
    z0 = COL_Z * LANES // HALF_INNER
    return pl.pallas_call(
        _outproj_kernel,
        grid=(t // tm,),
        in_specs=[
            pl.BlockSpec((tm, oa.shape[1]), lambda i: (i, 0)),
            pl.BlockSpec((tm, od.shape[1]), lambda i: (i, 0)),
            pl.BlockSpec((tm, C_INNER), lambda i: (i, 0)),
            pl.BlockSpec((tm, C_INNER), lambda i: (i, 0)),
            pl.BlockSpec((tm, HALF_INNER), lambda i: (i, z0)),
            pl.BlockSpec((tm, HALF_INNER), lambda i: (i, z0 + 1)),
            pl.BlockSpec((1, C_INNER), lambda i: (0, 0)),
            pl.BlockSpec((tm, d), lambda i: (i, 0)),
            pl.BlockSpec((None, N_MOD, d), lambda i: (mod_row(i * tm), 0, 0)),
            pl.BlockSpec((None, mw, d), lambda i: (layer, 0, 0), pipeline_mode=pl.Buffered(1)),
        ],
        out_specs=pl.BlockSpec((tm, d), lambda i: (i, 0)),
        out_shape=jax.ShapeDtypeStruct((t, d), F32),
        compiler_params=_cparams("parallel"),
        name="outproj",
    )(oa, od, yf, yb, p, p, ssm_norm, h, mod_l, w)


def _ffn_kernel(h_ref, mod_ref, nw_ref, wg_ref, wu_ref, wd_ref, o_ref, u_sc):
    f = pl.program_id(1)

    @pl.when(f == 0)
    def _():
        u_sc[...] = _modnorm(h_ref[...], nw_ref[...], mod_ref[3:4, :], mod_ref[4:5, :]).astype(BF16)
        o_ref[...] = jnp.zeros_like(o_ref)

    u = u_sc[...]
    g = _dot(u, wg_ref[...])
    a = (g * _sigmoid(g)) * _dot(u, wu_ref[...])
    o_ref[...] += _dot(a.astype(BF16), wd_ref[...])

    @pl.when(f == pl.num_programs(1) - 1)
    def _():
        o_ref[...] = h_ref[...] + mod_ref[5:6, :] * o_ref[...]


def _ffn_call(h, mod_l, nw, w_gu, w_d, layer, mod_row, tm, tf):
    t, d = h.shape
    ff = w_d.shape[1]
    nf = ff // tf
    return pl.pallas_call(
        _ffn_kernel,
        grid=(t // tm, nf),
        in_specs=[
            pl.BlockSpec((tm, d), lambda i, f: (i, 0)),
            pl.BlockSpec((None, N_MOD, d), lambda i, f: (mod_row(i * tm), 0, 0)),
            pl.BlockSpec((1, d), lambda i, f: (0, 0)),
            pl.BlockSpec((None, d, tf), lambda i, f: (layer, 0, f)),
            pl.BlockSpec((None, d, tf), lambda i, f: (layer, 0, nf + f)),
            pl.BlockSpec((None, tf, d), lambda i, f: (layer, f, 0)),
        ],
        out_specs=pl.BlockSpec((tm, d), lambda i, f: (i, 0)),
        out_shape=jax.ShapeDtypeStruct((t, d), F32),
        scratch_shapes=[pltpu.VMEM((tm, d), BF16)],
        compiler_params=_cparams("parallel", "arbitrary"),
        name="ffn",
    )(h, mod_l, nw, w_gu, w_gu, w_d)


def _final_norm_kernel(h_ref, w_ref, oc_ref, ol_ref, *, n_ctx_tiles):
    x = h_ref[...]
    y = x * lax.rsqrt(jnp.mean(x * x, axis=-1, keepdims=True) + EPS) * w_ref[...]
    i = pl.program_id(0)

    @pl.when(i < n_ctx_tiles)
    def _():
        oc_ref[...] = y

    @pl.when(i >= n_ctx_tiles)
    def _():
        ol_ref[...] = y


def _final_norm_call(h, w, n_ctx, tm):
    t, d = h.shape
    nc = n_ctx // tm
    return pl.pallas_call(
        functools.partial(_final_norm_kernel, n_ctx_tiles=nc),
        grid=(t // tm,),
        in_specs=[pl.BlockSpec((tm, d), lambda i: (i, 0)), pl.BlockSpec((1, d), lambda i: (0, 0))],
        out_specs=[pl.BlockSpec((tm, d), lambda i: (jnp.minimum(i, nc - 1), 0)),
                   pl.BlockSpec((tm, d), lambda i: (jnp.maximum(i - nc, 0), 0))],
        out_shape=[jax.ShapeDtypeStruct((n_ctx, d), F32), jax.ShapeDtypeStruct((t - n_ctx, d), F32)],
        compiler_params=_cparams("arbitrary"),
        name="final_norm",
    )(h, w)


def _rope_tables(n, rot_dim):
    rows = n // GRID_W
    row = jnp.repeat(jnp.arange(rows), GRID_W).astype(F32)
    col = (jnp.arange(rows * GRID_W) % GRID_W).astype(F32)
    quarter = rot_dim // 4
    inv = ROPE_BASE ** (-jnp.arange(quarter, dtype=F32) / quarter)
    ang = jnp.concatenate([row[:, None] * inv, col[:, None] * inv], axis=-1)
    c, s = jnp.cos(ang), jnp.sin(ang)
    reps = LANES // rot_dim
    return (jnp.tile(jnp.concatenate([c, c], axis=-1), (1, reps)),
            jnp.tile(jnp.concatenate([-s, s], axis=-1), (1, reps)))


def _lambda_init(layer):
    return 0.8 - 0.6 * math.exp(-0.3 * layer)


def _pad_lanes(v):
    return jnp.pad(v.reshape(v.shape[0], 1, -1), ((0, 0), (0, 0), (0, LANES - DT_COLS)))


def kernel(x_prompt, x_sample, cache_attn_k, cache_attn_v, cache_diff_k, cache_diff_v, state_ssm_fwd, state_ssm_bwd, c, c_ctx, w_ada, b_ada, norm_mix, norm_ffn, w_in, attn_sink, diff_lambda, diff_norm, conv_w, conv_b, dt_bias, a_log, d_skip, ssm_norm, w_out, w_gate_up, w_down, norm_final):
    batch, seq, d = x_prompt.shape
    dec_batch, dec_seq, _ = x_sample.shape
    depth = w_in.shape[0]
    past = cache_attn_k.shape[2]
    n_ctx = batch * seq
    dims = (batch, seq, dec_batch, dec_seq)
    tm = 512 if (n_ctx % 512 == 0 and dec_seq % 512 == 0) else 256
    tm_big = 1024 if (n_ctx % 1024 == 0 and dec_seq % 1024 == 0) else tm
    tq = min(256, dec_seq)
    assert n_ctx % dec_seq == 0 and n_ctx % tm == 0 and dec_seq % tm == 0
    assert seq % C_CHUNK == 0 and dec_seq % C_CHUNK == 0 and dec_seq % GRID_W == 0
    assert 1 + dec_batch <= 8

    def mod_row(start):
        return jnp.where(start < n_ctx, 0, 1 + (start - n_ctx) // dec_seq)

    w_in_b = w_in.astype(BF16)
    w_in_dt = jnp.pad(w_in_b[:, :, P_COLS:], ((0, 0), (0, 0), (0, LANES - DT_COLS)))
    w_out_b = w_out.astype(BF16)
    w_gu_b = w_gate_up.astype(BF16)
    w_down_b = w_down.astype(BF16)
    cos_a, sin_a = _rope_tables(dec_seq, HEAD_DIM)
    cos_d, sin_d = _rope_tables(dec_seq, B_HALF)
    conv_b3 = conv_b.reshape(depth, 1, C_CONV_CH)
    dtbias = _pad_lanes(dt_bias)
    alog = _pad_lanes(a_log)
    dskip_e = jnp.repeat(d_skip, C_HEAD_DIM, axis=-1).reshape(depth, 1, C_INNER)
    head_of_lane = jnp.arange(C_INNER) // C_HEAD_DIM
    e_f = (jnp.arange(LANES)[:, None] == head_of_lane[None, :]).astype(BF16)
    e_b = (jnp.arange(LANES)[:, None] == head_of_lane[None, :] + C_HEADS).astype(BF16)
    e3 = jnp.stack([jnp.concatenate([e_f] * 3, axis=0), jnp.concatenate([e_b] * 3, axis=0)])
    ck_a = cache_attn_k.reshape(dec_batch, depth, past, A_KV_HEADS * HEAD_DIM)
    cv_a = cache_attn_v.reshape(dec_batch, depth, past, A_KV_HEADS * HEAD_DIM)
    ck_d = cache_diff_k.reshape(dec_batch, depth, past, B_HEADS * HEAD_DIM)
    cv_d = cache_diff_v.reshape(dec_batch, depth, past, B_HEADS * HEAD_DIM)
    h0f = state_ssm_fwd.reshape(dec_batch, depth, C_INNER, C_STATE)
    h0b = state_ssm_bwd.reshape(dec_batch, depth, C_INNER, C_STATE)

    cond8 = jnp.concatenate([c_ctx[None, :], c, jnp.zeros((8 - 1 - dec_batch, d), F32)], axis=0)
    mod = _ada_call(cond8, w_ada, b_ada)[:, :1 + dec_batch].reshape(depth, 1 + dec_batch, N_MOD, d)

    h = jnp.concatenate([x_prompt.reshape(n_ctx, d), x_sample.reshape(dec_batch * dec_seq, d)], axis=0)
    ctx_out = []
    for l in range(depth):
        lam_init = _lambda_init(l)
        p, dt = _inproj_call(h, mod[l], norm_mix[l][None, :], w_in_b, w_in_dt, l, mod_row, tm_big)
        oa = _attn_a_calls(p, attn_sink, ck_a, cv_a, cos_a, sin_a, l, dims)
        od = _attn_b_calls(p, diff_lambda[l], diff_norm[l][None, :], ck_d, cv_d, cos_d, sin_d,
                           l, lam_init, dims, tq)
        xact = _conv_call(p, conv_w, conv_b3, l, dims)
        yf, yb, hf, hb = _ssd_call(xact, dt, dtbias[l], alog[l], dskip_e[l], e3, h0f, h0b, l, dims)
        h = _outproj_call(oa, od, yf, yb, p, ssm_norm[l][None, :], h, mod[l], w_out_b, l, mod_row, tm)
        h = _ffn_call(h, mod[l], norm_ffn[l][None, :], w_gu_b, w_down_b, l, mod_row, tm, 512)

        def ctx_cols(lo, hi, heads):
            return p[:n_ctx, lo * LANES:hi * LANES].reshape(batch, seq, heads, HEAD_DIM)

        ctx_out.append((ctx_cols(COL_KA, COL_VA, A_KV_HEADS), ctx_cols(COL_VA, COL_QD, A_KV_HEADS),
                        ctx_cols(COL_KD, COL_VD, B_HEADS), ctx_cols(COL_VD, COL_Z, B_HEADS),
                        hf.reshape(batch, C_HEADS, C_HEAD_DIM, C_STATE),
                        hb.reshape(batch, C_HEADS, C_HEAD_DIM, C_STATE)))

    y_prompt, y_sample = _final_norm_call(h, norm_final[None, :], n_ctx, tm)
    stacked = [jnp.stack([t[i] for t in ctx_out], axis=1) for i in range(6)]
    return (y_prompt.reshape(batch, seq, d), y_sample.reshape(dec_batch, dec_seq, d), *stacked)
```

```python
import functools
import math

import jax
import jax.numpy as jnp
from jax import lax
from jax.experimental import pallas as pl
from jax.experimental.pallas import tpu as pltpu

F32 = jnp.float32
BF16 = jnp.bfloat16

HEAD_DIM = 128
A_HEADS = 4
A_KV_HEADS = 2
A_GROUP = A_HEADS // A_KV_HEADS
BLOCK = 128
B_HEADS = 4
B_HALF = HEAD_DIM // 2
C_HEADS = 16
C_HEAD_DIM = 64
C_INNER = C_HEADS * C_HEAD_DIM
C_GROUPS = 2
C_STATE = 128
C_CONV = 5
C_CHUNK = 128
C_CONV_CH = C_INNER + 2 * C_GROUPS * C_STATE
GRID_W = 64
EPS = 1e-6
ROPE_BASE = 10000.0
N_MOD = 6
LANES = 128
HALO = 8
NEG = -1e30
LOG2E = math.log2(math.e)

COL_QA, COL_KA, COL_VA, COL_QD, COL_KD, COL_VD, COL_Z, COL_XBC = 0, 4, 6, 8, 12, 16, 20, 28
P_COLS = 40 * LANES
HALF_INNER = C_INNER // C_GROUPS
DT_COLS = 2 * C_HEADS

VMEM_LIMIT = 48 * 1024 * 1024


def _cparams(*sem, vmem=VMEM_LIMIT):
    return pltpu.CompilerParams(dimension_semantics=sem, vmem_limit_bytes=vmem)


def _dot(a, b):
    return jnp.dot(a, b, preferred_element_type=F32)


def _dot_nt(a, b):
    return lax.dot_general(a, b, (((1,), (1,)), ((), ())), preferred_element_type=F32)


def _sigmoid(x):
    return 1.0 / (1.0 + jnp.exp(-x))


def _split3(a):
    hi = a.astype(BF16)
    r = a - hi.astype(F32)
    mid = r.astype(BF16)
    lo = (r - mid.astype(F32)).astype(BF16)
    return hi, mid, lo


def _ada_kernel(cond_ref, w_ref, b_ref, o_ref):
    s = cond_ref[...]
    s = s * _sigmoid(s)
    o_ref[...] = _dot(s.astype(BF16), w_ref[...].astype(BF16)) + b_ref[...]


def _ada_call(cond8, w_ada, b_ada, tn=1024):
    depth, d, n = w_ada.shape
    return pl.pallas_call(
        _ada_kernel,
        grid=(depth, n // tn),
        in_specs=[
            pl.BlockSpec((8, d), lambda l, j: (0, 0)),
            pl.BlockSpec((None, d, tn), lambda l, j: (l, 0, j)),
            pl.BlockSpec((None, 1, tn), lambda l, j: (l, 0, j)),
        ],
        out_specs=pl.BlockSpec((None, 8, tn), lambda l, j: (l, 0, j)),
        out_shape=jax.ShapeDtypeStruct((depth, 8, n), F32),
        compiler_params=_cparams("parallel", "parallel"),
        name="adaln",
    )(cond8, w_ada, b_ada.reshape(depth, 1, n))


def _modnorm(x, nw, shift, scale):
    ms = jnp.mean(x * x, axis=-1, keepdims=True)
    y = x * lax.rsqrt(ms + EPS) * nw
    return y * (1.0 + scale) + shift


def _inproj_kernel(h_ref, mod_ref, nw_ref, w_ref, wdt_ref, p_ref, dt_ref, u_sc):
    @pl.when(pl.program_id(1) == 0)
    def _():
        u = _modnorm(h_ref[...], nw_ref[...], mod_ref[0:1, :], mod_ref[1:2, :]).astype(BF16)
        u_sc[...] = u
        dt_ref[...] = _dot(u, wdt_ref[...])

    p_ref[...] = _dot(u_sc[...], w_ref[...])


def _inproj_call(h, mod_l, nw, w, wdt, layer, mod_row, tm, tn=1024):
    t, d = h.shape
    return pl.pallas_call(
        _inproj_kernel,
        grid=(t // tm, P_COLS // tn),
        in_specs=[
            pl.BlockSpec((tm, d), lambda i, j: (i, 0)),
            pl.BlockSpec((None, N_MOD, d), lambda i, j: (mod_row(i * tm), 0, 0)),
            pl.BlockSpec((1, d), lambda i, j: (0, 0)),
            pl.BlockSpec((None, d, tn), lambda i, j: (layer, 0, j)),
            pl.BlockSpec((None, d, LANES), lambda i, j: (layer, 0, 0)),
        ],
        out_specs=[
            pl.BlockSpec((tm, tn), lambda i, j: (i, j)),
            pl.BlockSpec((tm, LANES), lambda i, j: (i, 0)),
        ],
        out_shape=[jax.ShapeDtypeStruct((t, P_COLS), F32), jax.ShapeDtypeStruct((t, LANES), F32)],
        scratch_shapes=[pltpu.VMEM((tm, d), BF16)],
        compiler_params=_cparams("parallel", "arbitrary"),
        name="inproj",
    )(h, mod_l, nw, w, wdt)


def _rope_a(x, c, s):
    return x * c + pltpu.roll(x, HEAD_DIM // 2, axis=1) * s


def _rope_d(x, c, s):
    lane = lax.broadcasted_iota(jnp.int32, x.shape, 1)
    q = B_HALF // 2
    partner = jnp.where((lane & (B_HALF - 1)) < q, pltpu.roll(x, LANES - q, axis=1), pltpu.roll(x, q, axis=1))
    return x * c + partner * s


def _sink_softmax_pv(s, sink, v):
    m = jnp.maximum(jnp.max(s, axis=-1, keepdims=True), sink)
    e = jnp.exp2(s - m)
    den = jnp.sum(e, axis=-1, keepdims=True) + jnp.exp2(sink - m)
    return _dot(e.astype(BF16), v) / den


def _attn_a_ctx_kernel(sink_ref, q_ref, k_ref, v_ref, buf_ref, o_ref, *, layer):
    del buf_ref
    scale = HEAD_DIM ** -0.5 * LOG2E
    for kv in range(A_KV_HEADS):
        ks = slice(kv * HEAD_DIM, (kv + 1) * HEAD_DIM)
        k = k_ref[:, ks].astype(BF16)
        v = v_ref[:, ks].astype(BF16)
        for g in range(A_GROUP):
            head = kv * A_GROUP + g
            sl = slice(head * HEAD_DIM, (head + 1) * HEAD_DIM)
            q = (q_ref[:, sl] * scale).astype(BF16)
            o = _sink_softmax_pv(_dot_nt(q, k), sink_ref[layer, head] * LOG2E, v)
            o_ref[:, sl] = o.astype(o_ref.dtype)


def _attn_a_lat_kernel(sink_ref, q_ref, kp_ref, kc_ref, kn_ref, vp_ref, vc_ref, vn_ref,
                       kx_ref, vx_ref, cq_ref, sq_ref, cp_ref, sp_ref, cn_ref, sn_ref,
                       ctx_rows_ref, o_ref, *, layer, n_steps):
    del ctx_rows_ref
    nb = pl.program_id(1)
    scale = HEAD_DIM ** -0.5 * LOG2E
    cq, sq = cq_ref[...], sq_ref[...]
    cp, sp, cn, sn = cp_ref[...], sp_ref[...], cn_ref[...], sn_ref[...]
    n_ctx = kx_ref.shape[0]
    tq = q_ref.shape[0]
    win = tq + 2 * BLOCK
    qi = lax.broadcasted_iota(jnp.int32, (tq, win + n_ctx), 0)
    kj = lax.broadcasted_iota(jnp.int32, (tq, win + n_ctx), 1)
    rel = kj - BLOCK - qi
    off_seq = ((kj < BLOCK) & (nb == 0)) | ((kj >= tq + BLOCK) & (nb == n_steps - 1))
    mask = jnp.logical_not((kj < win) & ((rel > BLOCK) | (rel < -BLOCK) | off_seq))
    for kv in range(A_KV_HEADS):
        ks = slice(kv * HEAD_DIM, (kv + 1) * HEAD_DIM)
        k_all = jnp.concatenate([
            _rope_a(kp_ref[:, ks], cp, sp).astype(BF16),
            _rope_a(kc_ref[:, ks], cq, sq).astype(BF16),
            _rope_a(kn_ref[:, ks], cn, sn).astype(BF16),
            kx_ref[:, ks].astype(BF16)], axis=0)
        v_all = jnp.concatenate([vp_ref[:, ks], vc_ref[:, ks], vn_ref[:, ks], vx_ref[:, ks]],
                                axis=0).astype(BF16)
        for g in range(A_GROUP):
            head = kv * A_GROUP + g
            sl = slice(head * HEAD_DIM, (head + 1) * HEAD_DIM)
            q = (_rope_a(q_ref[:, sl], cq, sq) * scale).astype(BF16)
            s = jnp.where(mask, _dot_nt(q, k_all), NEG)
            o = _sink_softmax_pv(s, sink_ref[layer, head] * LOG2E, v_all)
            o_ref[:, sl] = o.astype(o_ref.dtype)


def _attn_a_calls(p, sink, ck, cv, cos_a, sin_a, layer, dims):
    t = p.shape[0]
    batch, seq, dec_batch, dec_seq = dims
    n_ctx = batch * seq
    smem = pl.BlockSpec(memory_space=pltpu.SMEM)
    aw = A_HEADS * HEAD_DIM
    kw = A_KV_HEADS * HEAD_DIM
    qcol, kcol, vcol = COL_QA * LANES // aw, COL_KA * LANES // kw, COL_VA * LANES // kw
    oa_ctx = pl.pallas_call(
        functools.partial(_attn_a_ctx_kernel, layer=layer),
        grid=(batch,),
        in_specs=[
            smem,
            pl.BlockSpec((seq, aw), lambda b: (b, qcol)),
            pl.BlockSpec((seq, kw), lambda b: (b, kcol)),
            pl.BlockSpec((seq, kw), lambda b: (b, vcol)),
            pl.BlockSpec(memory_space=pl.ANY),
        ],
        out_specs=pl.BlockSpec((seq, aw), lambda b: (b, 0)),
        out_shape=jax.ShapeDtypeStruct((t, aw), BF16),
        input_output_aliases={4: 0},
        compiler_params=_cparams("parallel"),
        name="attn_a_ctx",
    )(sink, p, p, p, jnp.zeros((t, aw), BF16))

    nbl = dec_seq // BLOCK
    base = n_ctx // BLOCK
    qb = 2 if (nbl % 2 == 0 and n_ctx % (2 * BLOCK) == 0) else 1
    tq = qb * BLOCK
    steps = nbl // qb

    def run(b, n):
        return n_ctx // tq + b * steps + n

    def prev(n):
        return jnp.maximum(qb * n - 1, 0)

    def nxt(n):
        return jnp.minimum(qb * n + qb, nbl - 1)

    def blk(b, i):
        return base + b * nbl + i

    past = ck.shape[2]
    kblk = (BLOCK, kw)
    tblk = (BLOCK, HEAD_DIM)
    oa_lat = pl.pallas_call(
        functools.partial(_attn_a_lat_kernel, layer=layer, n_steps=steps),
        grid=(dec_batch, steps),
        in_specs=[
            smem,
            pl.BlockSpec((tq, aw), lambda b, n: (run(b, n), qcol)),
            pl.BlockSpec(kblk, lambda b, n: (blk(b, prev(n)), kcol)),
            pl.BlockSpec((tq, kw), lambda b, n: (run(b, n), kcol)),
            pl.BlockSpec(kblk, lambda b, n: (blk(b, nxt(n)), kcol)),
            pl.BlockSpec(kblk, lambda b, n: (blk(b, prev(n)), vcol)),
            pl.BlockSpec((tq, kw), lambda b, n: (run(b, n), vcol)),
            pl.BlockSpec(kblk, lambda b, n: (blk(b, nxt(n)), vcol)),
            pl.BlockSpec((None, None, past, kw), lambda b, n: (b, layer, 0, 0)),
            pl.BlockSpec((None, None, past, kw), lambda b, n: (b, layer, 0, 0)),
            pl.BlockSpec((tq, HEAD_DIM), lambda b, n: (n, 0)),
            pl.BlockSpec((tq, HEAD_DIM), lambda b, n: (n, 0)),
            pl.BlockSpec(tblk, lambda b, n: (prev(n), 0)),
            pl.BlockSpec(tblk, lambda b, n: (prev(n), 0)),
            pl.BlockSpec(tblk, lambda b, n: (nxt(n), 0)),
            pl.BlockSpec(tblk, lambda b, n: (nxt(n), 0)),
            pl.BlockSpec(memory_space=pl.ANY),
        ],
        out_specs=pl.BlockSpec((tq, aw), lambda b, n: (run(b, n), 0)),
        out_shape=jax.ShapeDtypeStruct((t, aw), BF16),
        input_output_aliases={16: 0},
        compiler_params=_cparams("parallel", "parallel"),
        name="attn_a_lat",
    )(sink, p, p, p, p, p, p, p, ck, cv, cos_a, sin_a, cos_a, sin_a, cos_a, sin_a, oa_ctx)
    return oa_lat


def _diff_lambda(lv, lam_init):
    a = jnp.sum(lv[0:1, :] * lv[1:2, :], axis=-1, keepdims=True)
    b = jnp.sum(lv[2:3, :] * lv[3:4, :], axis=-1, keepdims=True)
    return jnp.exp(a) - jnp.exp(b) + lam_init


def _diff_core(q, k, v, lam, dn_w, lam_init):
    lane = lax.broadcasted_iota(jnp.int32, q.shape, 1)
    qs = q * (B_HALF ** -0.5 * LOG2E)
    outs = []
    for half in range(2):
        sel = (lane < B_HALF) if half == 0 else (lane >= B_HALF)
        s = _dot_nt(jnp.where(sel, qs, 0.0).astype(BF16), k)
        e = jnp.exp2(s - jnp.max(s, axis=-1, keepdims=True))
        den = jnp.sum(e, axis=-1, keepdims=True)
        outs.append(_dot(e.astype(BF16), v) / den)
    o = outs[0] - lam * outs[1]
    ms = jnp.mean(o * o, axis=-1, keepdims=True)
    return o * lax.rsqrt(ms + EPS) * dn_w * (1.0 - lam_init)


def _attn_b_ctx_kernel(lv_ref, dn_ref, q_ref, k_ref, v_ref, buf_ref, o_ref, *, lam_init):
    del buf_ref
    lam = _diff_lambda(lv_ref[...], lam_init)
    for head in range(B_HEADS):
        sl = slice(head * HEAD_DIM, (head + 1) * HEAD_DIM)
        o = _diff_core(q_ref[:, sl], k_ref[:, sl].astype(BF16), v_ref[:, sl].astype(BF16), lam, dn_ref[...],
                       lam_init)
        o_ref[:, sl] = o.astype(o_ref.dtype)


def _attn_b_lat_kernel(lv_ref, dn_ref, q_ref, k_ref, v_ref, kx_ref, vx_ref, cq_ref, sq_ref,
                       ck_ref, sk_ref, ctx_rows_ref, o_ref, k_sc, v_sc, *, lam_init):
    del ctx_rows_ref
    n = k_ref.shape[0]

    @pl.when(pl.program_id(2) == 0)
    def _():
        k_sc[0:n, :] = _rope_d(k_ref[...], ck_ref[...], sk_ref[...]).astype(BF16)
        k_sc[n:, :] = kx_ref[...].astype(BF16)
        v_sc[0:n, :] = v_ref[...].astype(BF16)
        v_sc[n:, :] = vx_ref[...].astype(BF16)

    lam = _diff_lambda(lv_ref[...], lam_init)
    q = _rope_d(q_ref[...], cq_ref[...], sq_ref[...])
    o = _diff_core(q, k_sc[...], v_sc[...], lam, dn_ref[...], lam_init)
    o_ref[...] = o.astype(o_ref.dtype)


def _attn_b_calls(p, lv, dn_w, ck, cv, cos_d, sin_d, layer, lam_init, dims, tq):
    batch, seq, dec_batch, dec_seq = dims
    n_ctx = batch * seq
    hd = HEAD_DIM
    bw = B_HEADS * hd
    od_ctx = pl.pallas_call(
        functools.partial(_attn_b_ctx_kernel, lam_init=lam_init),
        grid=(batch,),
        in_specs=[
            pl.BlockSpec((4, B_HALF), lambda b: (0, 0)),
            pl.BlockSpec((1, hd), lambda b: (0, 0)),
            pl.BlockSpec((seq, bw), lambda b: (b, COL_QD * LANES // bw)),
            pl.BlockSpec((seq, bw), lambda b: (b, COL_KD * LANES // bw)),
            pl.BlockSpec((seq, bw), lambda b: (b, COL_VD * LANES // bw)),
            pl.BlockSpec(memory_space=pl.ANY),
        ],
        out_specs=pl.BlockSpec((seq, bw), lambda b: (b, 0)),
        out_shape=jax.ShapeDtypeStruct((p.shape[0], bw), BF16),
        input_output_aliases={5: 0},
        compiler_params=_cparams("parallel"),
        name="attn_b_ctx",
    )(lv, dn_w, p, p, p, jnp.zeros((p.shape[0], bw), BF16))

    past = ck.shape[2]
    nq = dec_seq // tq
    qbase = n_ctx // tq
    sbase = n_ctx // dec_seq
    od_lat = pl.pallas_call(
        functools.partial(_attn_b_lat_kernel, lam_init=lam_init),
        grid=(dec_batch, B_HEADS, nq),
        in_specs=[
            pl.BlockSpec((4, B_HALF), lambda b, h, i: (0, 0)),
            pl.BlockSpec((1, hd), lambda b, h, i: (0, 0)),
            pl.BlockSpec((tq, hd), lambda b, h, i: (qbase + b * nq + i, COL_QD + h)),
            pl.BlockSpec((dec_seq, hd), lambda b, h, i: (sbase + b, COL_KD + h)),
            pl.BlockSpec((dec_seq, hd), lambda b, h, i: (sbase + b, COL_VD + h)),
            pl.BlockSpec((None, None, past, hd), lambda b, h, i: (b, layer, 0, h)),
            pl.BlockSpec((None, None, past, hd), lambda b, h, i: (b, layer, 0, h)),
            pl.BlockSpec((tq, hd), lambda b, h, i: (i, 0)),
            pl.BlockSpec((tq, hd), lambda b, h, i: (i, 0)),
            pl.BlockSpec((dec_seq, hd), lambda b, h, i: (0, 0)),
            pl.BlockSpec((dec_seq, hd), lambda b, h, i: (0, 0)),
            pl.BlockSpec(memory_space=pl.ANY),
        ],
        out_specs=pl.BlockSpec((tq, hd), lambda b, h, i: (qbase + b * nq + i, h)),
        out_shape=jax.ShapeDtypeStruct((p.shape[0], B_HEADS * hd), BF16),
        input_output_aliases={11: 0},
        scratch_shapes=[pltpu.VMEM((dec_seq + past, hd), BF16), pltpu.VMEM((dec_seq + past, hd), BF16)],
        compiler_params=_cparams("parallel", "parallel", "arbitrary"),
        name="attn_b_lat",
    )(lv, dn_w, p, p, p, ck, cv, cos_d, sin_d, cos_d, sin_d, od_ctx)
    return od_lat


def _expand_heads(a, e3):
    return _dot(jnp.concatenate(_split3(a), axis=1), e3)


def _ssd_direction(xact, dt, a_row, e3, h_sc, hoff, fwd, dskip):
    L = C_CHUNK
    li = lax.broadcasted_iota(jnp.int32, (L, L), 0)
    si = lax.broadcasted_iota(jnp.int32, (L, L), 1)
    tri = (li >= si) if fwd else (li <= si)
    tri_b = jnp.where(tri, 1.0, 0.0).astype(BF16)
    dta = dt * a_row
    cum = _dot(jnp.concatenate([tri_b, tri_b, tri_b], axis=1), jnp.concatenate(_split3(dta), axis=0))
    cum_t = cum.T
    dt_t = dt.T
    end = cum[L - 1:L, :] if fwd else cum[0:1, :]
    to_end = jnp.exp(end - cum) * dt
    ecum = jnp.exp(cum)
    x = xact[:, :C_INNER]
    x_te = (x * _expand_heads(to_end, e3)).astype(BF16)
    dec_row = _expand_heads(jnp.broadcast_to(jnp.exp(end), (8, LANES)), e3)[0:1, :]
    lane = lax.broadcasted_iota(jnp.int32, (L, LANES), 1)
    lo = lane < C_HEAD_DIM
    gw = C_INNER // C_GROUPS
    ys, offs = [], []
    for g in range(C_GROUPS):
        bm = xact[:, C_INNER + g * C_STATE:C_INNER + (g + 1) * C_STATE]
        cm = xact[:, C_INNER + (C_GROUPS + g) * C_STATE:C_INNER + (C_GROUPS + g + 1) * C_STATE]
        cmb = cm.astype(BF16)
        cb = _dot_nt(cmb, bm.astype(BF16))
        h_prev = h_sc[:, g * gw:(g + 1) * gw]
        offs.append(_dot(cmb, h_prev.astype(BF16)))
        for pair in range(gw // LANES):
            blk = g * (gw // LANES) + pair
            w_parts = []
            for j in range(2):
                c = hoff + 2 * blk + j
                seg = cum[:, c:c + 1] - cum_t[c:c + 1, :]
                w = cb * jnp.exp(jnp.where(tri, seg, NEG)) * dt_t[c:c + 1, :]
                w_parts.append(w.astype(BF16))
            xb = x[:, blk * LANES:(blk + 1) * LANES]
            rhs = jnp.concatenate([jnp.where(lo, xb, 0.0), jnp.where(lo, 0.0, xb)], axis=0).astype(BF16)
            ys.append(_dot(jnp.concatenate(w_parts, axis=1), rhs))
        st = _dot(bm.T.astype(BF16), x_te[:, g * gw:(g + 1) * gw])
        h_sc[:, g * gw:(g + 1) * gw] = dec_row[:, g * gw:(g + 1) * gw] * h_prev + st
    y = jnp.concatenate(ys, axis=1) + jnp.concatenate(offs, axis=1) * _expand_heads(ecum, e3)
    if dskip is not None:
        y = y + dskip * x
    return y


def _conv_kernel(*refs, ncol, n_ctx_tiles, ctx_tps, lat_tps):
    x_refs, xp_refs, xn_refs = refs[:ncol], refs[ncol:2 * ncol], refs[2 * ncol:3 * ncol]
    cw_ref, cbias_ref, o_ref = refs[3 * ncol:]
    t = pl.program_id(0)
    is_ctx = t < n_ctx_tiles
    pos = jnp.where(is_ctx, t % ctx_tps, (t - n_ctx_tiles) % lat_tps)
    tps = jnp.where(is_ctx, ctx_tps, lat_tps)
    rows = o_ref.shape[0]

    def cols(rs):
        return jnp.concatenate([r[...] for r in rs], axis=1)

    xp = jnp.concatenate([jnp.where(pos > 0, cols(xp_refs), 0.0), cols(x_refs),
                          jnp.where(pos < tps - 1, cols(xn_refs), 0.0)], axis=0)
    n = rows + 2 * HALO
    acc = jnp.broadcast_to(cbias_ref[...], o_ref.shape)
    for k in range(C_CONV):
        shifted = xp if k == C_CONV // 2 else pltpu.roll(xp, (C_CONV // 2 - k) % n, axis=0)
        acc = acc + shifted[HALO:HALO + rows] * cw_ref[k:k + 1, :]
    o_ref[...] = acc * _sigmoid(acc)


def _conv_call(p, conv_w, conv_b, layer, dims, rows=512):
    batch, seq, dec_batch, dec_seq = dims
    t = p.shape[0]
    rows = min(rows, seq)
    assert seq % rows == 0 and dec_seq % rows == 0
    hpt = rows // HALO
    last_halo = t // HALO - 1
    cw = HALF_INNER
    c0 = COL_XBC * LANES // cw
    ncol = C_CONV_CH // cw
    main = [pl.BlockSpec((rows, cw), lambda i, c=c: (i, c0 + c)) for c in range(ncol)]
    prev = [pl.BlockSpec((HALO, cw), lambda i, c=c: (jnp.maximum(i * hpt - 1, 0), c0 + c)) for c in range(ncol)]
    nxt = [pl.BlockSpec((HALO, cw), lambda i, c=c: (jnp.minimum((i + 1) * hpt, last_halo), c0 + c))
           for c in range(ncol)]
    return pl.pallas_call(
        functools.partial(_conv_kernel, ncol=ncol, n_ctx_tiles=batch * seq // rows, ctx_tps=seq // rows,
                          lat_tps=dec_seq // rows),
        grid=(t // rows,),
        in_specs=main + prev + nxt + [
            pl.BlockSpec((None, C_CONV, C_CONV_CH), lambda i: (layer, 0, 0)),
            pl.BlockSpec((None, 1, C_CONV_CH), lambda i: (layer, 0, 0)),
        ],
        out_specs=pl.BlockSpec((rows, C_CONV_CH), lambda i: (i, 0)),
        out_shape=jax.ShapeDtypeStruct((t, C_CONV_CH), F32),
        compiler_params=_cparams("parallel"),
        name="conv",
    )(*([p] * (3 * ncol)), conv_w, conv_b)


def _ssd_kernel(xf_ref, xb_ref, dtf_ref, dtb_ref,
                dtbias_ref, alog_ref, dskip_ref, e3_ref, h0f_ref, h0b_ref,
                yf_ref, yb_ref, hfo_ref, hbo_ref, hf_sc, hb_sc,
                *, n_ctx_chunks, ctx_cps, lat_cps):
    s = pl.program_id(0)
    is_ctx = s < n_ctx_chunks
    pos = jnp.where(is_ctx, s % ctx_cps, (s - n_ctx_chunks) % lat_cps)
    cps = jnp.where(is_ctx, ctx_cps, lat_cps)
    first = pos == 0
    last = pos == cps - 1

    @pl.when(first & is_ctx)
    def _():
        hf_sc[...] = jnp.zeros_like(hf_sc)
        hb_sc[...] = jnp.zeros_like(hb_sc)

    @pl.when(first & jnp.logical_not(is_ctx))
    def _():
        hf_sc[...] = h0f_ref[...].T
        hb_sc[...] = h0b_ref[...].T

    def softplus(v):
        return jnp.maximum(v, 0.0) + jnp.log(1.0 + jnp.exp(-jnp.abs(v)))

    a_row = -jnp.exp(alog_ref[...])
    e3 = e3_ref[...]

    group = xf_ref.shape[0] // C_CHUNK
    for c in range(group):
        rows = slice(c * C_CHUNK, (c + 1) * C_CHUNK)
        dt = softplus(dtf_ref[rows, :] + dtbias_ref[...])
        yf_ref[rows, :] = _ssd_direction(xf_ref[rows, :], dt, a_row, e3[0], hf_sc, 0, True, dskip_ref[...])
    for c in reversed(range(group)):
        rows = slice(c * C_CHUNK, (c + 1) * C_CHUNK)
        dt = softplus(dtb_ref[rows, :] + dtbias_ref[...])
        yb_ref[rows, :] = _ssd_direction(xb_ref[rows, :], dt, a_row, e3[1], hb_sc, C_HEADS, False, None)

    @pl.when(last & is_ctx)
    def _():
        hfo_ref[...] = hf_sc[...].T.reshape(hfo_ref.shape)
        hbo_ref[...] = hb_sc[...].T.reshape(hbo_ref.shape)


def _ssd_call(xact, dt, dtbias, alog, dskip_e, e3, h0f, h0b, layer, dims):
    batch, seq, dec_batch, dec_seq = dims
    group = 2 if (seq % (2 * C_CHUNK) == 0 and dec_seq % (2 * C_CHUNK) == 0) else 1
    L = group * C_CHUNK
    ctx_cps, lat_cps = seq // L, dec_seq // L
    n_ctx_chunks = batch * ctx_cps
    n_chunks = n_ctx_chunks + dec_batch * lat_cps
    t = xact.shape[0]

    def mirror(s):
        c_ctx = (s // ctx_cps) * ctx_cps + (ctx_cps - 1 - s % ctx_cps)
        r = s - n_ctx_chunks
        c_lat = n_ctx_chunks + (r // lat_cps) * lat_cps + (lat_cps - 1 - r % lat_cps)
        return jnp.where(s < n_ctx_chunks, c_ctx, c_lat)

    def lat_b(s):
        return jnp.maximum(s - n_ctx_chunks, 0) // lat_cps

    def ctx_b(s):
        return jnp.minimum(s // ctx_cps, batch - 1)

    const2 = lambda s: (0, 0)
    hspec = pl.BlockSpec((None, None, C_INNER, C_STATE), lambda s: (lat_b(s), layer, 0, 0))
    ospec = pl.BlockSpec((None, C_HEADS, C_HEAD_DIM, C_STATE), lambda s: (ctx_b(s), 0, 0, 0))
    return pl.pallas_call(
        functools.partial(_ssd_kernel, n_ctx_chunks=n_ctx_chunks, ctx_cps=ctx_cps, lat_cps=lat_cps),
        grid=(n_chunks,),
        in_specs=[
            pl.BlockSpec((L, C_CONV_CH), lambda s: (s, 0)),
            pl.BlockSpec((L, C_CONV_CH), lambda s: (mirror(s), 0)),
            pl.BlockSpec((L, LANES), lambda s: (s, 0)),
            pl.BlockSpec((L, LANES), lambda s: (mirror(s), 0)),
            pl.BlockSpec((1, LANES), const2),
            pl.BlockSpec((1, LANES), const2),
            pl.BlockSpec((1, C_INNER), const2),
            pl.BlockSpec((2, 3 * LANES, C_INNER), lambda s: (0, 0, 0)),
            hspec, hspec,
        ],
        out_specs=[
            pl.BlockSpec((L, C_INNER), lambda s: (s, 0)),
            pl.BlockSpec((L, C_INNER), lambda s: (mirror(s), 0)),
            ospec, ospec,
        ],
        out_shape=[
            jax.ShapeDtypeStruct((t, C_INNER), F32), jax.ShapeDtypeStruct((t, C_INNER), F32),
            jax.ShapeDtypeStruct((batch, C_HEADS, C_HEAD_DIM, C_STATE), F32),
            jax.ShapeDtypeStruct((batch, C_HEADS, C_HEAD_DIM, C_STATE), F32),
        ],
        scratch_shapes=[
            pltpu.VMEM((C_STATE, C_INNER), F32),
            pltpu.VMEM((C_STATE, C_INNER), F32),
        ],
        compiler_params=_cparams("arbitrary"),
        name="ssd",
    )(xact, xact, dt, dt, dtbias, alog, dskip_e, e3, h0f, h0b)


def _outproj_kernel(oa_ref, od_ref, yf_ref, yb_ref, z0_ref, z1_ref, sn_ref, h_ref, mod_ref, w_ref, o_ref):
    na = oa_ref.shape[1]
    nd = od_ref.shape[1]
    acc = _dot(oa_ref[...], w_ref[0:na, :]) + _dot(od_ref[...], w_ref[na:na + nd, :])
    gw = HALF_INNER
    for g, z_ref in enumerate((z0_ref, z1_ref)):
        z = z_ref[...]
        yg = (yf_ref[:, g * gw:(g + 1) * gw] + yb_ref[:, g * gw:(g + 1) * gw]) * (z * _sigmoid(z))
        yg = yg * lax.rsqrt(jnp.mean(yg * yg, axis=-1, keepdims=True) + EPS)
        yg = yg * sn_ref[:, g * gw:(g + 1) * gw]
        lo = na + nd + g * gw
        acc = acc + _dot(yg.astype(BF16), w_ref[lo:lo + gw, :])
    o_ref[...] = h_ref[...] + mod_ref[2:3, :] * acc


def _outproj_call(oa, od, yf, yb, p, ssm_norm, h, mod_l, w, layer, mod_row, tm):
    t, d = h.shape
    mw = w.shape[1]
    z0 = COL_Z * LANES // HALF_INNER
    return pl.pallas_call(
        _outproj_kernel,
        grid=(t // tm,),
        in_specs=[
            pl.BlockSpec((tm, oa.shape[1]), lambda i: (i, 0)),
            pl.BlockSpec((tm, od.shape[1]), lambda i: (i, 0)),
            pl.BlockSpec((tm, C_INNER), lambda i: (i, 0)),
            pl.BlockSpec((tm, C_INNER), lambda i: (i, 0)),
            pl.BlockSpec((tm, HALF_INNER), lambda i: (i, z0)),
            pl.BlockSpec((tm, HALF_INNER), lambda i: (i, z0 + 1)),
            pl.BlockSpec((1, C_INNER), lambda i: (0, 0)),
            pl.BlockSpec((tm, d), lambda i: (i, 0)),
            pl.BlockSpec((None, N_MOD, d), lambda i: (mod_row(i * tm), 0, 0)),
            pl.BlockSpec((None, mw, d), lambda i: (layer, 0, 0), pipeline_mode=pl.Buffered(1)),
        ],
        out_specs=pl.BlockSpec((tm, d), lambda i: (i, 0)),
        out_shape=jax.ShapeDtypeStruct((t, d), F32),
        compiler_params=_cparams("parallel"),
        name="outproj",
    )(oa, od, yf, yb, p, p, ssm_norm, h, mod_l, w)


def _ffn_kernel(h_ref, mod_ref, nw_ref, wg_ref, wu_ref, wd_ref, o_ref, u_sc):
    f = pl.program_id(1)

    @pl.when(f == 0)
    def _():
        u_sc[...] = _modnorm(h_ref[...], nw_ref[...], mod_ref[3:4, :], mod_ref[4:5, :]).astype(BF16)
        o_ref[...] = jnp.zeros_like(o_ref)

    u = u_sc[...]
    g = _dot(u, wg_ref[...])
    a = (g * _sigmoid(g)) * _dot(u, wu_ref[...])
    o_ref[...] += _dot(a.astype(BF16), wd_ref[...])

    @pl.when(f == pl.num_programs(1) - 1)
    def _():
        o_ref[...] = h_ref[...] + mod_ref[5:6, :] * o_ref[...]


def _ffn_call(h, mod_l, nw, w_gu, w_d, layer, mod_row, tm, tf):
    t, d = h.shape
    ff = w_d.shape[1]
    nf = ff // tf
    return pl.pallas_call(
        _ffn_kernel,
        grid=(t // tm, nf),
        in_specs=[
            pl.BlockSpec((tm, d), lambda i, f: (i, 0)),
            pl.BlockSpec((None, N_MOD, d), lambda i, f: (mod_row(i * tm), 0, 0)),
            pl.BlockSpec((1, d), lambda i, f: (0, 0)),
            pl.BlockSpec((None, d, tf), lambda i, f: (layer, 0, f)),
            pl.BlockSpec((None, d, tf), lambda i, f: (layer, 0, nf + f)),
            pl.BlockSpec((None, tf, d), lambda i, f: (layer, f, 0)),
        ],
        out_specs=pl.BlockSpec((tm, d), lambda i, f: (i, 0)),
        out_shape=jax.ShapeDtypeStruct((t, d), F32),
        scratch_shapes=[pltpu.VMEM((tm, d), BF16)],
        compiler_params=_cparams("parallel", "arbitrary"),
        name="ffn",
    )(h, mod_l, nw, w_gu, w_gu, w_d)


def _final_norm_kernel(h_ref, w_ref, oc_ref, ol_ref, *, n_ctx_tiles):
    x = h_ref[...]
    y = x * lax.rsqrt(jnp.mean(x * x, axis=-1, keepdims=True) + EPS) * w_ref[...]
    i = pl.program_id(0)

    @pl.when(i < n_ctx_tiles)
    def _():
        oc_ref[...] = y

    @pl.when(i >= n_ctx_tiles)
    def _():
        ol_ref[...] = y


def _final_norm_call(h, w, n_ctx, tm):
    t, d = h.shape
    nc = n_ctx // tm
    return pl.pallas_call(
        functools.partial(_final_norm_kernel, n_ctx_tiles=nc),
        grid=(t // tm,),
        in_specs=[pl.BlockSpec((tm, d), lambda i: (i, 0)), pl.BlockSpec((1, d), lambda i: (0, 0))],
        out_specs=[pl.BlockSpec((tm, d), lambda i: (jnp.minimum(i, nc - 1), 0)),
                   pl.BlockSpec((tm, d), lambda i: (jnp.maximum(i - nc, 0), 0))],
        out_shape=[jax.ShapeDtypeStruct((n_ctx, d), F32), jax.ShapeDtypeStruct((t - n_ctx, d), F32)],
        compiler_params=_cparams("arbitrary"),
        name="final_norm",
    )(h, w)


def _rope_tables(n, rot_dim):
    rows = n // GRID_W
    row = jnp.repeat(jnp.arange(rows), GRID_W).astype(F32)
    col = (jnp.arange(rows * GRID_W) % GRID_W).astype(F32)
    quarter = rot_dim // 4
    inv = ROPE_BASE ** (-jnp.arange(quarter, dtype=F32) / quarter)
    ang = jnp.concatenate([row[:, None] * inv, col[:, None] * inv], axis=-1)
    c, s = jnp.cos(ang), jnp.sin(ang)
    reps = LANES // rot_dim
    return (jnp.tile(jnp.concatenate([c, c], axis=-1), (1, reps)),
            jnp.tile(jnp.concatenate([-s, s], axis=-1), (1, reps)))


def _lambda_init(layer):
    return 0.8 - 0.6 * math.exp(-0.3 * layer)


def _pad_lanes(v):
    return jnp.pad(v.reshape(v.shape[0], 1, -1), ((0, 0), (0, 0), (0, LANES - DT_COLS)))


def kernel(x_prompt, x_sample, cache_attn_k, cache_attn_v, cache_diff_k, cache_diff_v, state_ssm_fwd, state_ssm_bwd, c, c_ctx, w_ada, b_ada, norm_mix, norm_ffn, w_in, attn_sink, diff_lambda, diff_norm, conv_w, conv_b, dt_bias, a_log, d_skip, ssm_norm, w_out, w_gate_up, w_down, norm_final):
    batch, seq, d = x_prompt.shape
    dec_batch, dec_seq, _ = x_sample.shape
    depth = w_in.shape[0]
    past = cache_attn_k.shape[2]
    n_ctx = batch * seq
    dims = (batch, seq, dec_batch, dec_seq)
    tm = 512 if (n_ctx % 512 == 0 and dec_seq % 512 == 0) else 256
    tm_big = 1024 if (n_ctx % 1024 == 0 and dec_seq % 1024 == 0) else tm
    tq = min(256, dec_seq)
    assert n_ctx % dec_seq == 0 and n_ctx % tm == 0 and dec_seq % tm == 0
    assert seq % C_CHUNK == 0 and dec_seq % C_CHUNK == 0 and dec_seq % GRID_W == 0
    assert 1 + dec_batch <= 8

    def mod_row(start):
        return jnp.where(start < n_ctx, 0, 1 + (start - n_ctx) // dec_seq)

    w_in_b = w_in.astype(BF16)
    w_in_dt = jnp.pad(w_in_b[:, :, P_COLS:], ((0, 0), (0, 0), (0, LANES - DT_COLS)))
    w_out_b = w_out.astype(BF16)
    w_gu_b = w_gate_up.astype(BF16)
    w_down_b = w_down.astype(BF16)
    cos_a, sin_a = _rope_tables(dec_seq, HEAD_DIM)
    cos_d, sin_d = _rope_tables(dec_seq, B_HALF)
    conv_b3 = conv_b.reshape(depth, 1, C_CONV_CH)
    dtbias = _pad_lanes(dt_bias)
    alog = _pad_lanes(a_log)
    dskip_e = jnp.repeat(d_skip, C_HEAD_DIM, axis=-1).reshape(depth, 1, C_INNER)
    head_of_lane = jnp.arange(C_INNER) // C_HEAD_DIM
    e_f = (jnp.arange(LANES)[:, None] == head_of_lane[None, :]).astype(BF16)
    e_b = (jnp.arange(LANES)[:, None] == head_of_lane[None, :] + C_HEADS).astype(BF16)
    e3 = jnp.stack([jnp.concatenate([e_f] * 3, axis=0), jnp.concatenate([e_b] * 3, axis=0)])
    ck_a = cache_attn_k.reshape(dec_batch, depth, past, A_KV_HEADS * HEAD_DIM)
    cv_a = cache_attn_v.reshape(dec_batch, depth, past, A_KV_HEADS * HEAD_DIM)
    ck_d = cache_diff_k.reshape(dec_batch, depth, past, B_HEADS * HEAD_DIM)
    cv_d = cache_diff_v.reshape(dec_batch, depth, past, B_HEADS * HEAD_DIM)
    h0f = state_ssm_fwd.reshape(dec_batch, depth, C_INNER, C_STATE)
    h0b = state_ssm_bwd.reshape(dec_batch, depth, C_INNER, C_STATE)

    cond8 = jnp.concatenate([c_ctx[None, :], c, jnp.zeros((8 - 1 - dec_batch, d), F32)], axis=0)
    mod = _ada_call(cond8, w_ada, b_ada)[:, :1 + dec_batch].reshape(depth, 1 + dec_batch, N_MOD, d)

    h = jnp.concatenate([x_prompt.reshape(n_ctx, d), x_sample.reshape(dec_batch * dec_seq, d)], axis=0)
    ctx_out = []
    for l in range(depth):
        lam_init = _lambda_init(l)
        p, dt = _inproj_call(h, mod[l], norm_mix[l][None, :], w_in_b, w_in_dt, l, mod_row, tm_big)
        oa = _attn_a_calls(p, attn_sink, ck_a, cv_a, cos_a, sin_a, l, dims)
        od = _attn_b_calls(p, diff_lambda[l], diff_norm[l][None, :], ck_d, cv_d, cos_d, sin_d,
                           l, lam_init, dims, tq)
        xact = _conv_call(p, conv_w, conv_b3, l, dims)
        yf, yb, hf, hb = _ssd_call(xact, dt, dtbias[l], alog[l], dskip_e[l], e3, h0f, h0b, l, dims)
        h = _outproj_call(oa, od, yf, yb, p, ssm_norm[l][None, :], h, mod[l], w_out_b, l, mod_row, tm)
        h = _ffn_call(h, mod[l], norm_ffn[l][None, :], w_gu_b, w_down_b, l, mod_row, tm, 512)

        def ctx_cols(lo, hi, heads):
            return p[:n_ctx, lo * LANES:hi * LANES].reshape(batch, seq, heads, HEAD_DIM)

        ctx_out.append((ctx_cols(COL_KA, COL_VA, A_KV_HEADS), ctx_cols(COL_VA, COL_QD, A_KV_HEADS),
                        ctx_cols(COL_KD, COL_VD, B_HEADS), ctx_cols(COL_VD, COL_Z, B_HEADS),
                        hf, hb))

    y_prompt, y_sample = _final_norm_call(h, norm_final[None, :], n_ctx, tm)
    stacked = [jnp.stack([t[i] for t in ctx_out], axis=1) for i in range(6)]
    return (y_prompt.reshape(batch, seq, d), y_sample.reshape(dec_batch, dec_seq, d), *stacked)
```

```python
import functools
import math

import jax
import jax.numpy as jnp
from jax import lax
from jax.experimental import pallas as pl
from jax.experimental.pallas import tpu as pltpu

F32 = jnp.float32
BF16 = jnp.bfloat16

HEAD_DIM = 128
A_HEADS = 4
A_KV_HEADS = 2
A_GROUP = A_HEADS // A_KV_HEADS
BLOCK = 128
B_HEADS = 4
B_HALF = HEAD_DIM // 2
C_HEADS = 16
C_HEAD_DIM = 64
C_INNER = C_HEADS * C_HEAD_DIM
C_GROUPS = 2
C_STATE = 128
C_CONV = 5
C_CHUNK = 128
C_CONV_CH = C_INNER + 2 * C_GROUPS * C_STATE
GRID_W = 64
EPS = 1e-6
ROPE_BASE = 10000.0
N_MOD = 6
LANES = 128
HALO = 16
NEG = -1e30
LOG2E = math.log2(math.e)

COL_QA, COL_KA, COL_VA, COL_QD, COL_KD, COL_VD, COL_Z, COL_XBC = 0, 4, 6, 8, 12, 16, 20, 28
P_COLS = 40 * LANES
HALF_INNER = C_INNER // C_GROUPS
DT_COLS = 2 * C_HEADS

VMEM_LIMIT = 48 * 1024 * 1024


def _cparams(*sem, vmem=VMEM_LIMIT):
    return pltpu.CompilerParams(dimension_semantics=sem, vmem_limit_bytes=vmem)


def _dot(a, b):
    return jnp.dot(a, b, preferred_element_type=F32)


def _dot_nt(a, b):
    return lax.dot_general(a, b, (((1,), (1,)), ((), ())), preferred_element_type=F32)


def _sigmoid(x):
    return 1.0 / (1.0 + jnp.exp(-x))


def _split3(a):
    hi = a.astype(BF16)
    r = a - hi.astype(F32)
    mid = r.astype(BF16)
    lo = (r - mid.astype(F32)).astype(BF16)
    return hi, mid, lo


def _ada_kernel(cond_ref, w_ref, b_ref, o_ref):
    s = cond_ref[...]
    s = s * _sigmoid(s)
    o_ref[...] = _dot(s.astype(BF16), w_ref[...].astype(BF16)) + b_ref[...]


def _ada_call(cond8, w_ada, b_ada, tn=1024):
    depth, d, n = w_ada.shape
    return pl.pallas_call(
        _ada_kernel,
        grid=(depth, n // tn),
        in_specs=[
            pl.BlockSpec((8, d), lambda l, j: (0, 0)),
            pl.BlockSpec((None, d, tn), lambda l, j: (l, 0, j)),
            pl.BlockSpec((None, 1, tn), lambda l, j: (l, 0, j)),
        ],
        out_specs=pl.BlockSpec((None, 8, tn), lambda l, j: (l, 0, j)),
        out_shape=jax.ShapeDtypeStruct((depth, 8, n), F32),
        compiler_params=_cparams("parallel", "parallel"),
        name="adaln",
    )(cond8, w_ada, b_ada.reshape(depth, 1, n))


def _modnorm(x, nw, shift, scale):
    ms = jnp.mean(x * x, axis=-1, keepdims=True)
    y = x * lax.rsqrt(ms + EPS) * nw
    return y * (1.0 + scale) + shift


def _inproj_kernel(h_ref, mod_ref, nw_ref, w_ref, wdt_ref, p_ref, dt_ref, kv_ref, u_sc, *, n_ctx_tiles, kv_cols):
    i = pl.program_id(0)
    j = pl.program_id(1)

    @pl.when(j == 0)
    def _():
        u = _modnorm(h_ref[...], nw_ref[...], mod_ref[0:1, :], mod_ref[1:2, :]).astype(BF16)
        u_sc[...] = u
        dt_ref[...] = _dot(u, wdt_ref[...])

    res = _dot(u_sc[...], w_ref[...])
    p_ref[...] = res.astype(p_ref.dtype)

    tn = res.shape[1]
    kvw = kv_ref.shape[1]
    for piece, col in enumerate(kv_cols):
        assert col // tn == piece

        @pl.when((i < n_ctx_tiles) & (j == piece))
        def _():
            kv_ref[...] = res[:, col % tn:col % tn + kvw]

    @pl.when((i == n_ctx_tiles) & (j < len(kv_cols)))
    def _():
        kv_ref[...] = jnp.zeros_like(kv_ref)


def _inproj_call(h, mod_l, nw, w, wdt, mod_row, n_ctx, tm, tn=1024):
    t, d = h.shape
    nc = n_ctx // tm
    kvw = HALF_INNER
    kv_cols = (COL_KA * LANES, COL_KD * LANES, COL_VD * LANES)
    last = len(kv_cols) - 1
    return pl.pallas_call(
        functools.partial(_inproj_kernel, n_ctx_tiles=nc, kv_cols=kv_cols),
        grid=(t // tm, P_COLS // tn),
        in_specs=[
            pl.BlockSpec((tm, d), lambda i, j: (i, 0)),
            pl.BlockSpec((None, N_MOD, d), lambda i, j: (mod_row(i * tm), 0, 0)),
            pl.BlockSpec((1, d), lambda i, j: (0, 0)),
            pl.BlockSpec((d, tn), lambda i, j: (0, j)),
            pl.BlockSpec((d, LANES), lambda i, j: (0, 0)),
        ],
        out_specs=[
            pl.BlockSpec((tm, tn), lambda i, j: (i, j)),
            pl.BlockSpec((tm, LANES), lambda i, j: (i, 0)),
            pl.BlockSpec((tm, kvw), lambda i, j: (jnp.minimum(i, nc), jnp.where(i <= nc, jnp.minimum(j, last), last))),
        ],
        out_shape=[jax.ShapeDtypeStruct((t, P_COLS), BF16), jax.ShapeDtypeStruct((t, LANES), F32),
                   jax.ShapeDtypeStruct((n_ctx + tm, kvw * len(kv_cols)), F32)],
        scratch_shapes=[pltpu.VMEM((tm, d), BF16)],
        compiler_params=_cparams("arbitrary", "arbitrary"),
        name="inproj",
    )(h, mod_l, nw, w, wdt)


def _rope_a(x, c, s):
    x = x.astype(F32)
    return x * c + pltpu.roll(x, HEAD_DIM // 2, axis=1) * s


def _rope_d(x, c, s):
    x = x.astype(F32)
    lane = lax.broadcasted_iota(jnp.int32, x.shape, 1)
    q = B_HALF // 2
    partner = jnp.where((lane & (B_HALF - 1)) < q, pltpu.roll(x, LANES - q, axis=1), pltpu.roll(x, q, axis=1))
    return x * c + partner * s


def _sink_softmax_pv(s, sink, v):
    m = jnp.maximum(jnp.max(s, axis=-1, keepdims=True), sink)
    e = jnp.exp2(s - m)
    den = jnp.sum(e, axis=-1, keepdims=True) + jnp.exp2(sink - m)
    return _dot(e.astype(BF16), v) / den


def _attn_a_ctx_kernel(sink_ref, q_ref, k_ref, v_ref, buf_ref, o_ref, *, layer):
    del buf_ref
    scale = HEAD_DIM ** -0.5 * LOG2E
    for kv in range(A_KV_HEADS):
        ks = slice(kv * HEAD_DIM, (kv + 1) * HEAD_DIM)
        k = k_ref[:, ks].astype(BF16)
        v = v_ref[:, ks].astype(BF16)
        for g in range(A_GROUP):
            head = kv * A_GROUP + g
            sl = slice(head * HEAD_DIM, (head + 1) * HEAD_DIM)
            q = (q_ref[:, sl].astype(F32) * scale).astype(BF16)
            o = _sink_softmax_pv(_dot_nt(q, k), sink_ref[layer, head] * LOG2E, v)
            o_ref[:, sl] = o.astype(o_ref.dtype)


def _attn_a_lat_kernel(sink_ref, q_ref, kp_ref, kc_ref, kn_ref, vp_ref, vc_ref, vn_ref,
                       kx_ref, vx_ref, cq_ref, sq_ref, cp_ref, sp_ref, cn_ref, sn_ref,
                       ctx_rows_ref, o_ref, *, layer, n_steps):
    del ctx_rows_ref
    nb = pl.program_id(1)
    scale = HEAD_DIM ** -0.5 * LOG2E
    cq, sq = cq_ref[...], sq_ref[...]
    cp, sp, cn, sn = cp_ref[...], sp_ref[...], cn_ref[...], sn_ref[...]
    n_ctx = kx_ref.shape[0]
    tq = q_ref.shape[0]
    win = tq + 2 * BLOCK
    qi = lax.broadcasted_iota(jnp.int32, (tq, win + n_ctx), 0)
    kj = lax.broadcasted_iota(jnp.int32, (tq, win + n_ctx), 1)
    rel = kj - BLOCK - qi
    off_seq = ((kj < BLOCK) & (nb == 0)) | ((kj >= tq + BLOCK) & (nb == n_steps - 1))
    mask = jnp.logical_not((kj < win) & ((rel > BLOCK) | (rel < -BLOCK) | off_seq))
    for kv in range(A_KV_HEADS):
        ks = slice(kv * HEAD_DIM, (kv + 1) * HEAD_DIM)
        k_all = jnp.concatenate([
            _rope_a(kp_ref[:, ks], cp, sp).astype(BF16),
            _rope_a(kc_ref[:, ks], cq, sq).astype(BF16),
            _rope_a(kn_ref[:, ks], cn, sn).astype(BF16),
            kx_ref[:, ks].astype(BF16)], axis=0)
        v_all = jnp.concatenate([r[:, ks].astype(BF16) for r in (vp_ref, vc_ref, vn_ref, vx_ref)], axis=0)
        for g in range(A_GROUP):
            head = kv * A_GROUP + g
            sl = slice(head * HEAD_DIM, (head + 1) * HEAD_DIM)
            q = (_rope_a(q_ref[:, sl], cq, sq) * scale).astype(BF16)
            s = jnp.where(mask, _dot_nt(q, k_all), NEG)
            o = _sink_softmax_pv(s, sink_ref[layer, head] * LOG2E, v_all)
            o_ref[:, sl] = o.astype(o_ref.dtype)


def _attn_a_calls(p, sink, ck, cv, cos_a, sin_a, layer, dims):
    t = p.shape[0]
    batch, seq, dec_batch, dec_seq = dims
    n_ctx = batch * seq
    smem = pl.BlockSpec(memory_space=pltpu.SMEM)
    aw = A_HEADS * HEAD_DIM
    kw = A_KV_HEADS * HEAD_DIM
    qcol, kcol, vcol = COL_QA * LANES // aw, COL_KA * LANES // kw, COL_VA * LANES // kw
    oa_ctx = pl.pallas_call(
        functools.partial(_attn_a_ctx_kernel, layer=layer),
        grid=(batch,),
        in_specs=[
            smem,
            pl.BlockSpec((seq, aw), lambda b: (b, qcol)),
            pl.BlockSpec((seq, kw), lambda b: (b, kcol)),
            pl.BlockSpec((seq, kw), lambda b: (b, vcol)),
            pl.BlockSpec(memory_space=pl.ANY),
        ],
        out_specs=pl.BlockSpec((seq, aw), lambda b: (b, 0)),
        out_shape=jax.ShapeDtypeStruct((t, aw), BF16),
        input_output_aliases={4: 0},
        compiler_params=_cparams("parallel"),
        name="attn_a_ctx",
    )(sink, p, p, p, jnp.zeros((t, aw), BF16))

    nbl = dec_seq // BLOCK
    base = n_ctx // BLOCK
    qb = 2 if (nbl % 2 == 0 and n_ctx % (2 * BLOCK) == 0) else 1
    tq = qb * BLOCK
    steps = nbl // qb

    def run(b, n):
        return n_ctx // tq + b * steps + n

    def prev(n):
        return jnp.maximum(qb * n - 1, 0)

    def nxt(n):
        return jnp.minimum(qb * n + qb, nbl - 1)

    def blk(b, i):
        return base + b * nbl + i

    past = ck.shape[2]
    kblk = (BLOCK, kw)
    tblk = (BLOCK, HEAD_DIM)
    oa_lat = pl.pallas_call(
        functools.partial(_attn_a_lat_kernel, layer=layer, n_steps=steps),
        grid=(dec_batch, steps),
        in_specs=[
            smem,
            pl.BlockSpec((tq, aw), lambda b, n: (run(b, n), qcol)),
            pl.BlockSpec(kblk, lambda b, n: (blk(b, prev(n)), kcol)),
            pl.BlockSpec((tq, kw), lambda b, n: (run(b, n), kcol)),
            pl.BlockSpec(kblk, lambda b, n: (blk(b, nxt(n)), kcol)),
            pl.BlockSpec(kblk, lambda b, n: (blk(b, prev(n)), vcol)),
            pl.BlockSpec((tq, kw), lambda b, n: (run(b, n), vcol)),
            pl.BlockSpec(kblk, lambda b, n: (blk(b, nxt(n)), vcol)),
            pl.BlockSpec((None, None, past, kw), lambda b, n: (b, layer, 0, 0)),
            pl.BlockSpec((None, None, past, kw), lambda b, n: (b, layer, 0, 0)),
            pl.BlockSpec((tq, HEAD_DIM), lambda b, n: (n, 0)),
            pl.BlockSpec((tq, HEAD_DIM), lambda b, n: (n, 0)),
            pl.BlockSpec(tblk, lambda b, n: (prev(n), 0)),
            pl.BlockSpec(tblk, lambda b, n: (prev(n), 0)),
            pl.BlockSpec(tblk, lambda b, n: (nxt(n), 0)),
            pl.BlockSpec(tblk, lambda b, n: (nxt(n), 0)),
            pl.BlockSpec(memory_space=pl.ANY),
        ],
        out_specs=pl.BlockSpec((tq, aw), lambda b, n: (run(b, n), 0)),
        out_shape=jax.ShapeDtypeStruct((t, aw), BF16),
        input_output_aliases={16: 0},
        compiler_params=_cparams("parallel", "parallel"),
        name="attn_a_lat",
    )(sink, p, p, p, p, p, p, p, ck, cv, cos_a, sin_a, cos_a, sin_a, cos_a, sin_a, oa_ctx)
    return oa_lat


def _diff_lambda(lv, lam_init):
    a = jnp.sum(lv[0:1, :] * lv[1:2, :], axis=-1, keepdims=True)
    b = jnp.sum(lv[2:3, :] * lv[3:4, :], axis=-1, keepdims=True)
    return jnp.exp(a) - jnp.exp(b) + lam_init


def _diff_core(q, k, v, lam, dn_w, lam_init):
    lane = lax.broadcasted_iota(jnp.int32, q.shape, 1)
    qs = q.astype(F32) * (B_HALF ** -0.5 * LOG2E)
    outs = []
    for half in range(2):
        sel = (lane < B_HALF) if half == 0 else (lane >= B_HALF)
        s = _dot_nt(jnp.where(sel, qs, 0.0).astype(BF16), k)
        e = jnp.exp2(s - jnp.max(s, axis=-1, keepdims=True))
        den = jnp.sum(e, axis=-1, keepdims=True)
        outs.append(_dot(e.astype(BF16), v) / den)
    o = outs[0] - lam * outs[1]
    ms = jnp.mean(o * o, axis=-1, keepdims=True)
    return o * lax.rsqrt(ms + EPS) * dn_w * (1.0 - lam_init)


def _attn_b_ctx_kernel(lv_ref, dn_ref, q_ref, k_ref, v_ref, buf_ref, o_ref, *, lam_init):
    del buf_ref
    lam = _diff_lambda(lv_ref[...], lam_init)
    for head in range(B_HEADS):
        sl = slice(head * HEAD_DIM, (head + 1) * HEAD_DIM)
        o = _diff_core(q_ref[:, sl], k_ref[:, sl].astype(BF16), v_ref[:, sl].astype(BF16), lam, dn_ref[...],
                       lam_init)
        o_ref[:, sl] = o.astype(o_ref.dtype)


def _attn_b_lat_kernel(lv_ref, dn_ref, q_ref, k_ref, v_ref, kx_ref, vx_ref, cq_ref, sq_ref,
                       ck_ref, sk_ref, ctx_rows_ref, *rest, lam_init):
    del ctx_rows_ref
    if len(rest) == 7:
        nwi_ref, nwo_ref, o_ref, nwi_out, nwo_out, k_sc, v_sc = rest
        nwi_out[...] = nwi_ref[...].astype(BF16)
        nwo_out[...] = nwo_ref[...].astype(BF16)
    else:
        o_ref, k_sc, v_sc = rest
    n = k_ref.shape[0]

    @pl.when(pl.program_id(2) == 0)
    def _():
        k_sc[0:n, :] = _rope_d(k_ref[...], ck_ref[...], sk_ref[...]).astype(BF16)
        k_sc[n:, :] = kx_ref[...].astype(BF16)
        v_sc[0:n, :] = v_ref[...].astype(BF16)
        v_sc[n:, :] = vx_ref[...].astype(BF16)

    lam = _diff_lambda(lv_ref[...], lam_init)
    q = _rope_d(q_ref[...], cq_ref[...], sq_ref[...])
    o = _diff_core(q, k_sc[...], v_sc[...], lam, dn_ref[...], lam_init)
    o_ref[...] = o.astype(o_ref.dtype)


def _attn_b_calls(p, lv, dn_w, ck, cv, cos_d, sin_d, layer, lam_init, dims, tq, next_f32=None):
    batch, seq, dec_batch, dec_seq = dims
    n_ctx = batch * seq
    hd = HEAD_DIM
    bw = B_HEADS * hd
    od_ctx = pl.pallas_call(
        functools.partial(_attn_b_ctx_kernel, lam_init=lam_init),
        grid=(batch,),
        in_specs=[
            pl.BlockSpec((4, B_HALF), lambda b: (0, 0)),
            pl.BlockSpec((1, hd), lambda b: (0, 0)),
            pl.BlockSpec((seq, bw), lambda b: (b, COL_QD * LANES // bw)),
            pl.BlockSpec((seq, bw), lambda b: (b, COL_KD * LANES // bw)),
            pl.BlockSpec((seq, bw), lambda b: (b, COL_VD * LANES // bw)),
            pl.BlockSpec(memory_space=pl.ANY),
        ],
        out_specs=pl.BlockSpec((seq, bw), lambda b: (b, 0)),
        out_shape=jax.ShapeDtypeStruct((p.shape[0], bw), BF16),
        input_output_aliases={5: 0},
        compiler_params=_cparams("parallel"),
        name="attn_b_ctx",
    )(lv, dn_w, p, p, p, jnp.zeros((p.shape[0], bw), BF16))

    past = ck.shape[2]
    nq = dec_seq // tq
    qbase = n_ctx // tq
    sbase = n_ctx // dec_seq
    in_specs = [
        pl.BlockSpec((4, B_HALF), lambda b, h, i: (0, 0)),
        pl.BlockSpec((1, hd), lambda b, h, i: (0, 0)),
        pl.BlockSpec((tq, hd), lambda b, h, i: (qbase + b * nq + i, COL_QD + h)),
        pl.BlockSpec((dec_seq, hd), lambda b, h, i: (sbase + b, COL_KD + h)),
        pl.BlockSpec((dec_seq, hd), lambda b, h, i: (sbase + b, COL_VD + h)),
        pl.BlockSpec((None, None, past, hd), lambda b, h, i: (b, layer, 0, h)),
        pl.BlockSpec((None, None, past, hd), lambda b, h, i: (b, layer, 0, h)),
        pl.BlockSpec((tq, hd), lambda b, h, i: (i, 0)),
        pl.BlockSpec((tq, hd), lambda b, h, i: (i, 0)),
        pl.BlockSpec((dec_seq, hd), lambda b, h, i: (0, 0)),
        pl.BlockSpec((dec_seq, hd), lambda b, h, i: (0, 0)),
        pl.BlockSpec(memory_space=pl.ANY),
    ]
    out_specs = [pl.BlockSpec((tq, hd), lambda b, h, i: (qbase + b * nq + i, h))]
    out_shape = [jax.ShapeDtypeStruct((p.shape[0], B_HEADS * hd), BF16)]
    args = [lv, dn_w, p, p, p, ck, cv, cos_d, sin_d, cos_d, sin_d, od_ctx]
    if next_f32 is not None:
        w_in32, w_out32, nl = next_f32
        steps = dec_batch * B_HEADS * nq
        d = w_in32.shape[1]
        assert d % steps == 0 and (d // steps) % 16 == 0 and w_out32.shape[1] % steps == 0
        ri, ro = d // steps, w_out32.shape[1] // steps

        def slab(b, h, i):
            return (b * B_HEADS + h) * nq + i

        in_specs += [pl.BlockSpec((None, ri, w_in32.shape[2]), lambda b, h, i: (nl, slab(b, h, i), 0)),
                     pl.BlockSpec((None, ro, w_out32.shape[2]), lambda b, h, i: (nl, slab(b, h, i), 0))]
        out_specs += [pl.BlockSpec((ri, w_in32.shape[2]), lambda b, h, i: (slab(b, h, i), 0)),
                      pl.BlockSpec((ro, w_out32.shape[2]), lambda b, h, i: (slab(b, h, i), 0))]
        out_shape += [jax.ShapeDtypeStruct(w_in32.shape[1:], BF16), jax.ShapeDtypeStruct(w_out32.shape[1:], BF16)]
        args += [w_in32, w_out32]
    res = pl.pallas_call(
        functools.partial(_attn_b_lat_kernel, lam_init=lam_init),
        grid=(dec_batch, B_HEADS, nq),
        in_specs=in_specs,
        out_specs=out_specs,
        out_shape=out_shape,
        input_output_aliases={11: 0},
        scratch_shapes=[pltpu.VMEM((dec_seq + past, hd), BF16), pltpu.VMEM((dec_seq + past, hd), BF16)],
        compiler_params=_cparams("parallel", "parallel", "arbitrary"),
        name="attn_b_lat",
    )(*args)
    return res


def _expand_heads(a, e3):
    return _dot(jnp.concatenate(_split3(a), axis=1), e3)


def _ssd_direction(xact, dt, a_row, e3, h_sc, hoff, fwd, dskip):
    L = C_CHUNK
    li = lax.broadcasted_iota(jnp.int32, (L, L), 0)
    si = lax.broadcasted_iota(jnp.int32, (L, L), 1)
    tri = (li >= si) if fwd else (li <= si)
    tri_b = jnp.where(tri, 1.0, 0.0).astype(BF16)
    dta = dt * a_row
    cum = _dot(jnp.concatenate([tri_b, tri_b, tri_b], axis=1), jnp.concatenate(_split3(dta), axis=0))
    cum_t = cum.T
    dt_t = dt.T
    end = cum[L - 1:L, :] if fwd else cum[0:1, :]
    to_end = jnp.exp(end - cum) * dt
    ecum = jnp.exp(cum)
    x = xact[:, :C_INNER]
    x_te = (x * _expand_heads(to_end, e3)).astype(BF16)
    dec_row = _expand_heads(jnp.broadcast_to(jnp.exp(end), (8, LANES)), e3)[0:1, :]
    lane = lax.broadcasted_iota(jnp.int32, (L, LANES), 1)
    lo = lane < C_HEAD_DIM
    gw = C_INNER // C_GROUPS
    ys, offs = [], []
    for g in range(C_GROUPS):
        bm = xact[:, C_INNER + g * C_STATE:C_INNER + (g + 1) * C_STATE]
        cm = xact[:, C_INNER + (C_GROUPS + g) * C_STATE:C_INNER + (C_GROUPS + g + 1) * C_STATE]
        cmb = cm.astype(BF16)
        cb = _dot_nt(cmb, bm.astype(BF16))
        h_prev = h_sc[:, g * gw:(g + 1) * gw]
        offs.append(_dot(cmb, h_prev.astype(BF16)))
        for pair in range(gw // LANES):
            blk = g * (gw // LANES) + pair
            w_parts = []
            for j in range(2):
                c = hoff + 2 * blk + j
                seg = cum[:, c:c + 1] - cum_t[c:c + 1, :]
                w = cb * jnp.exp(jnp.where(tri, seg, NEG)) * dt_t[c:c + 1, :]
                w_parts.append(w.astype(BF16))
            xb = x[:, blk * LANES:(blk + 1) * LANES]
            rhs = jnp.concatenate([jnp.where(lo, xb, 0.0), jnp.where(lo, 0.0, xb)], axis=0).astype(BF16)
            ys.append(_dot(jnp.concatenate(w_parts, axis=1), rhs))
        st = _dot(bm.T.astype(BF16), x_te[:, g * gw:(g + 1) * gw])
        h_sc[:, g * gw:(g + 1) * gw] = dec_row[:, g * gw:(g + 1) * gw] * h_prev + st
    y = jnp.concatenate(ys, axis=1) + jnp.concatenate(offs, axis=1) * _expand_heads(ecum, e3)
    if dskip is not None:
        y = y + dskip * x
    return y


def _conv_kernel(*refs, ncol, n_ctx_tiles, ctx_tps, lat_tps):
    x_refs, xp_refs, xn_refs = refs[:ncol], refs[ncol:2 * ncol], refs[2 * ncol:3 * ncol]
    cw_ref, cbias_ref, o_ref = refs[3 * ncol:]
    t = pl.program_id(0)
    is_ctx = t < n_ctx_tiles
    pos = jnp.where(is_ctx, t % ctx_tps, (t - n_ctx_tiles) % lat_tps)
    tps = jnp.where(is_ctx, ctx_tps, lat_tps)
    rows = o_ref.shape[0]

    def cols(rs):
        return jnp.concatenate([r[...].astype(F32) for r in rs], axis=1)

    xp = jnp.concatenate([jnp.where(pos > 0, cols(xp_refs), 0.0), cols(x_refs),
                          jnp.where(pos < tps - 1, cols(xn_refs), 0.0)], axis=0)
    n = rows + 2 * HALO
    acc = jnp.broadcast_to(cbias_ref[...], o_ref.shape)
    for k in range(C_CONV):
        shifted = xp if k == C_CONV // 2 else pltpu.roll(xp, (C_CONV // 2 - k) % n, axis=0)
        acc = acc + shifted[HALO:HALO + rows] * cw_ref[k:k + 1, :]
    o_ref[...] = acc * _sigmoid(acc)


def _conv_call(p, conv_w, conv_b, layer, dims, rows=512):
    batch, seq, dec_batch, dec_seq = dims
    t = p.shape[0]
    rows = min(rows, seq)
    assert seq % rows == 0 and dec_seq % rows == 0
    hpt = rows // HALO
    last_halo = t // HALO - 1
    cw = HALF_INNER
    c0 = COL_XBC * LANES // cw
    ncol = C_CONV_CH // cw
    main = [pl.BlockSpec((rows, cw), lambda i, c=c: (i, c0 + c)) for c in range(ncol)]
    prev = [pl.BlockSpec((HALO, cw), lambda i, c=c: (jnp.maximum(i * hpt - 1, 0), c0 + c)) for c in range(ncol)]
    nxt = [pl.BlockSpec((HALO, cw), lambda i, c=c: (jnp.minimum((i + 1) * hpt, last_halo), c0 + c))
           for c in range(ncol)]
    return pl.pallas_call(
        functools.partial(_conv_kernel, ncol=ncol, n_ctx_tiles=batch * seq // rows, ctx_tps=seq // rows,
                          lat_tps=dec_seq // rows),
        grid=(t // rows,),
        in_specs=main + prev + nxt + [
            pl.BlockSpec((None, C_CONV, C_CONV_CH), lambda i: (layer, 0, 0)),
            pl.BlockSpec((None, 1, C_CONV_CH), lambda i: (layer, 0, 0)),
        ],
        out_specs=pl.BlockSpec((rows, C_CONV_CH), lambda i: (i, 0)),
        out_shape=jax.ShapeDtypeStruct((t, C_CONV_CH), F32),
        compiler_params=_cparams("parallel"),
        name="conv",
    )(*([p] * (3 * ncol)), conv_w, conv_b)


def _ssd_kernel(xf_ref, xb_ref, dtf_ref, dtb_ref,
                dtbias_ref, alog_ref, dskip_ref, e3_ref, h0f_ref, h0b_ref,
                yf_ref, yb_ref, hfo_ref, hbo_ref, hf_sc, hb_sc,
                *, n_ctx_chunks, ctx_cps, lat_cps):
    s = pl.program_id(0)
    is_ctx = s < n_ctx_chunks
    pos = jnp.where(is_ctx, s % ctx_cps, (s - n_ctx_chunks) % lat_cps)
    cps = jnp.where(is_ctx, ctx_cps, lat_cps)
    first = pos == 0
    last = pos == cps - 1

    @pl.when(first & is_ctx)
    def _():
        hf_sc[...] = jnp.zeros_like(hf_sc)
        hb_sc[...] = jnp.zeros_like(hb_sc)

    @pl.when(first & jnp.logical_not(is_ctx))
    def _():
        hf_sc[...] = h0f_ref[...].T
        hb_sc[...] = h0b_ref[...].T

    def softplus(v):
        return jnp.maximum(v, 0.0) + jnp.log(1.0 + jnp.exp(-jnp.abs(v)))

    a_row = -jnp.exp(alog_ref[...])
    e3 = e3_ref[...]

    group = xf_ref.shape[0] // C_CHUNK
    for c in range(group):
        rows = slice(c * C_CHUNK, (c + 1) * C_CHUNK)
        dt = softplus(dtf_ref[rows, :] + dtbias_ref[...])
        yf = _ssd_direction(xf_ref[rows, :], dt, a_row, e3[0], hf_sc, 0, True, dskip_ref[...])
        yf_ref[rows, :] = yf.astype(yf_ref.dtype)
    for c in reversed(range(group)):
        rows = slice(c * C_CHUNK, (c + 1) * C_CHUNK)
        dt = softplus(dtb_ref[rows, :] + dtbias_ref[...])
        yb = _ssd_direction(xb_ref[rows, :], dt, a_row, e3[1], hb_sc, C_HEADS, False, None)
        yb_ref[rows, :] = yb.astype(yb_ref.dtype)

    @pl.when(last & is_ctx)
    def _():
        hfo_ref[...] = hf_sc[...].T.reshape(hfo_ref.shape)
        hbo_ref[...] = hb_sc[...].T.reshape(hbo_ref.shape)


def _ssd_call(xact, dt, dtbias, alog, dskip_e, e3, h0f, h0b, layer, dims):
    batch, seq, dec_batch, dec_seq = dims
    group = 2 if (seq % (2 * C_CHUNK) == 0 and dec_seq % (2 * C_CHUNK) == 0) else 1
    L = group * C_CHUNK
    ctx_cps, lat_cps = seq // L, dec_seq // L
    n_ctx_chunks = batch * ctx_cps
    n_chunks = n_ctx_chunks + dec_batch * lat_cps
    t = xact.shape[0]

    def mirror(s):
        c_ctx = (s // ctx_cps) * ctx_cps + (ctx_cps - 1 - s % ctx_cps)
        r = s - n_ctx_chunks
        c_lat = n_ctx_chunks + (r // lat_cps) * lat_cps + (lat_cps - 1 - r % lat_cps)
        return jnp.where(s < n_ctx_chunks, c_ctx, c_lat)

    def lat_b(s):
        return jnp.maximum(s - n_ctx_chunks, 0) // lat_cps

    def ctx_b(s):
        return jnp.minimum(s // ctx_cps, batch - 1)

    const2 = lambda s: (0, 0)
    hspec = pl.BlockSpec((None, None, C_INNER, C_STATE), lambda s: (lat_b(s), layer, 0, 0))
    ospec = pl.BlockSpec((None, C_HEADS, C_HEAD_DIM, C_STATE), lambda s: (ctx_b(s), 0, 0, 0))
    return pl.pallas_call(
        functools.partial(_ssd_kernel, n_ctx_chunks=n_ctx_chunks, ctx_cps=ctx_cps, lat_cps=lat_cps),
        grid=(n_chunks,),
        in_specs=[
            pl.BlockSpec((L, C_CONV_CH), lambda s: (s, 0)),
            pl.BlockSpec((L, C_CONV_CH), lambda s: (mirror(s), 0)),
            pl.BlockSpec((L, LANES), lambda s: (s, 0)),
            pl.BlockSpec((L, LANES), lambda s: (mirror(s), 0)),
            pl.BlockSpec((1, LANES), const2),
            pl.BlockSpec((1, LANES), const2),
            pl.BlockSpec((1, C_INNER), const2),
            pl.BlockSpec((2, 3 * LANES, C_INNER), lambda s: (0, 0, 0)),
            hspec, hspec,
        ],
        out_specs=[
            pl.BlockSpec((L, C_INNER), lambda s: (s, 0)),
            pl.BlockSpec((L, C_INNER), lambda s: (mirror(s), 0)),
            ospec, ospec,
        ],
        out_shape=[
            jax.ShapeDtypeStruct((t, C_INNER), BF16), jax.ShapeDtypeStruct((t, C_INNER), BF16),
            jax.ShapeDtypeStruct((batch, C_HEADS, C_HEAD_DIM, C_STATE), F32),
            jax.ShapeDtypeStruct((batch, C_HEADS, C_HEAD_DIM, C_STATE), F32),
        ],
        scratch_shapes=[
            pltpu.VMEM((C_STATE, C_INNER), F32),
            pltpu.VMEM((C_STATE, C_INNER), F32),
        ],
        compiler_params=_cparams("arbitrary"),
        name="ssd",
    )(xact, xact, dt, dt, dtbias, alog, dskip_e, e3, h0f, h0b)


def _outproj_kernel(oa_ref, od_ref, yf_ref, yb_ref, z0_ref, z1_ref, sn_ref, h_ref, mod_ref, w_ref, o_ref):
    na = oa_ref.shape[1]
    nd = od_ref.shape[1]
    acc = _dot(oa_ref[...], w_ref[0:na, :]) + _dot(od_ref[...], w_ref[na:na + nd, :])
    gw = HALF_INNER
    for g, z_ref in enumerate((z0_ref, z1_ref)):
        z = z_ref[...].astype(F32)
        y = yf_ref[:, g * gw:(g + 1) * gw].astype(F32) + yb_ref[:, g * gw:(g + 1) * gw].astype(F32)
        yg = y * (z * _sigmoid(z))
        yg = yg * lax.rsqrt(jnp.mean(yg * yg, axis=-1, keepdims=True) + EPS)
        yg = yg * sn_ref[:, g * gw:(g + 1) * gw]
        lo = na + nd + g * gw
        acc = acc + _dot(yg.astype(BF16), w_ref[lo:lo + gw, :])
    o_ref[...] = h_ref[...] + mod_ref[2:3, :] * acc


def _outproj_call(oa, od, yf, yb, p, ssm_norm, h, mod_l, w, mod_row, tm):
    t, d = h.shape
    mw = w.shape[0]
    z0 = COL_Z * LANES // HALF_INNER
    return pl.pallas_call(
        _outproj_kernel,
        grid=(t // tm,),
        in_specs=[
            pl.BlockSpec((tm, oa.shape[1]), lambda i: (i, 0)),
            pl.BlockSpec((tm, od.shape[1]), lambda i: (i, 0)),
            pl.BlockSpec((tm, C_INNER), lambda i: (i, 0)),
            pl.BlockSpec((tm, C_INNER), lambda i: (i, 0)),
            pl.BlockSpec((tm, HALF_INNER), lambda i: (i, z0)),
            pl.BlockSpec((tm, HALF_INNER), lambda i: (i, z0 + 1)),
            pl.BlockSpec((1, C_INNER), lambda i: (0, 0)),
            pl.BlockSpec((tm, d), lambda i: (i, 0)),
            pl.BlockSpec((None, N_MOD, d), lambda i: (mod_row(i * tm), 0, 0)),
            pl.BlockSpec((mw, d), lambda i: (0, 0), pipeline_mode=pl.Buffered(1)),
        ],
        out_specs=pl.BlockSpec((tm, d), lambda i: (i, 0)),
        out_shape=jax.ShapeDtypeStruct((t, d), F32),
        compiler_params=_cparams("parallel"),
        name="outproj",
    )(oa, od, yf, yb, p, p, ssm_norm, h, mod_l, w)


def _ffn_kernel(h_ref, mod_ref, nw_ref, wg_ref, wu_ref, wd_ref, *rest):
    if len(rest) == 6:
        ngu_ref, nd_ref, o_ref, ngu_out, nd_out, u_sc = rest
        ngu_out[...] = ngu_ref[...].astype(BF16)
        nd_out[...] = nd_ref[...].astype(BF16)
    else:
        o_ref, u_sc = rest
    f = pl.program_id(1)

    @pl.when(f == 0)
    def _():
        u_sc[...] = _modnorm(h_ref[...], nw_ref[...], mod_ref[3:4, :], mod_ref[4:5, :]).astype(BF16)
        o_ref[...] = jnp.zeros_like(o_ref)

    u = u_sc[...]
    g = _dot(u, wg_ref[...])
    a = (g * _sigmoid(g)) * _dot(u, wu_ref[...])
    o_ref[...] += _dot(a.astype(BF16), wd_ref[...])

    @pl.when(f == pl.num_programs(1) - 1)
    def _():
        o_ref[...] = h_ref[...] + mod_ref[5:6, :] * o_ref[...]


def _ffn_call(h, mod_l, nw, w_gu, w_d, mod_row, tm, tf, next_f32=None):
    t, d = h.shape
    ff = w_d.shape[0]
    nf = ff // tf
    ni = t // tm
    in_specs = [
        pl.BlockSpec((tm, d), lambda i, f: (i, 0)),
        pl.BlockSpec((None, N_MOD, d), lambda i, f: (mod_row(i * tm), 0, 0)),
        pl.BlockSpec((1, d), lambda i, f: (0, 0)),
        pl.BlockSpec((d, tf), lambda i, f: (0, f)),
        pl.BlockSpec((d, tf), lambda i, f: (0, nf + f)),
        pl.BlockSpec((tf, d), lambda i, f: (f, 0)),
    ]
    out_specs = [pl.BlockSpec((tm, d), lambda i, f: (i, 0))]
    out_shape = [jax.ShapeDtypeStruct((t, d), F32)]
    args = [h, mod_l, nw, w_gu, w_gu, w_d]
    if next_f32 is not None:
        w_gu32, w_d32, nl = next_f32
        nr = 1 << ((ni - 1).bit_length() - 1)
        assert nr < ni and d % (nr * LANES) == 0
        slab = d // nr

        def src(i):
            return jnp.minimum(i, nr - 1)

        def dst(i):
            return jnp.minimum(i, nr)

        def fcol(i, f):
            return jnp.where(i <= nr, f, nf - 1)

        in_specs += [pl.BlockSpec((None, slab, 2 * tf), lambda i, f: (nl, src(i), f)),
                     pl.BlockSpec((None, tf, slab), lambda i, f: (nl, f, src(i)))]
        out_specs += [pl.BlockSpec((slab, 2 * tf), lambda i, f: (dst(i), fcol(i, f))),
                      pl.BlockSpec((tf, slab), lambda i, f: (fcol(i, f), dst(i)))]
        out_shape += [jax.ShapeDtypeStruct((d + slab, 2 * ff), BF16), jax.ShapeDtypeStruct((ff, d + slab), BF16)]
        args += [w_gu32, w_d32]
    return pl.pallas_call(
        _ffn_kernel,
        grid=(ni, nf),
        in_specs=in_specs,
        out_specs=out_specs,
        out_shape=out_shape,
        scratch_shapes=[pltpu.VMEM((tm, d), BF16)],
        compiler_params=_cparams("arbitrary", "arbitrary"),
        name="ffn",
    )(*args)


def _final_norm_kernel(h_ref, w_ref, oc_ref, ol_ref, *, n_ctx_tiles):
    x = h_ref[...]
    y = x * lax.rsqrt(jnp.mean(x * x, axis=-1, keepdims=True) + EPS) * w_ref[...]
    i = pl.program_id(0)

    @pl.when(i < n_ctx_tiles)
    def _():
        oc_ref[...] = y

    @pl.when(i >= n_ctx_tiles)
    def _():
        ol_ref[...] = y


def _final_norm_call(h, w, n_ctx, tm):
    t, d = h.shape
    nc = n_ctx // tm
    return pl.pallas_call(
        functools.partial(_final_norm_kernel, n_ctx_tiles=nc),
        grid=(t // tm,),
        in_specs=[pl.BlockSpec((tm, d), lambda i: (i, 0)), pl.BlockSpec((1, d), lambda i: (0, 0))],
        out_specs=[pl.BlockSpec((tm, d), lambda i: (jnp.minimum(i, nc - 1), 0)),
                   pl.BlockSpec((tm, d), lambda i: (jnp.maximum(i - nc, 0), 0))],
        out_shape=[jax.ShapeDtypeStruct((n_ctx, d), F32), jax.ShapeDtypeStruct((t - n_ctx, d), F32)],
        compiler_params=_cparams("arbitrary"),
        name="final_norm",
    )(h, w)


def _rope_tables(n, rot_dim):
    rows = n // GRID_W
    row = jnp.repeat(jnp.arange(rows), GRID_W).astype(F32)
    col = (jnp.arange(rows * GRID_W) % GRID_W).astype(F32)
    quarter = rot_dim // 4
    inv = ROPE_BASE ** (-jnp.arange(quarter, dtype=F32) / quarter)
    ang = jnp.concatenate([row[:, None] * inv, col[:, None] * inv], axis=-1)
    c, s = jnp.cos(ang), jnp.sin(ang)
    reps = LANES // rot_dim
    return (jnp.tile(jnp.concatenate([c, c], axis=-1), (1, reps)),
            jnp.tile(jnp.concatenate([-s, s], axis=-1), (1, reps)))


def _lambda_init(layer):
    return 0.8 - 0.6 * math.exp(-0.3 * layer)


def _pad_lanes(v):
    return jnp.pad(v.reshape(v.shape[0], 1, -1), ((0, 0), (0, 0), (0, LANES - DT_COLS)))


def kernel(x_prompt, x_sample, cache_attn_k, cache_attn_v, cache_diff_k, cache_diff_v, state_ssm_fwd, state_ssm_bwd, c, c_ctx, w_ada, b_ada, norm_mix, norm_ffn, w_in, attn_sink, diff_lambda, diff_norm, conv_w, conv_b, dt_bias, a_log, d_skip, ssm_norm, w_out, w_gate_up, w_down, norm_final):
    batch, seq, d = x_prompt.shape
    dec_batch, dec_seq, _ = x_sample.shape
    depth = w_in.shape[0]
    past = cache_attn_k.shape[2]
    n_ctx = batch * seq
    dims = (batch, seq, dec_batch, dec_seq)
    tm = 512 if (n_ctx % 512 == 0 and dec_seq % 512 == 0) else 256
    tm_big = 1024 if (n_ctx % 1024 == 0 and dec_seq % 1024 == 0) else tm
    tq = min(256, dec_seq)
    assert n_ctx % dec_seq == 0 and n_ctx % tm == 0 and dec_seq % tm == 0
    assert seq % C_CHUNK == 0 and dec_seq % C_CHUNK == 0 and dec_seq % GRID_W == 0
    assert 1 + dec_batch <= 8

    def mod_row(start):
        return jnp.where(start < n_ctx, 0, 1 + (start - n_ctx) // dec_seq)

    w_in_b = w_in[0].astype(BF16)
    w_out_b = w_out[0].astype(BF16)
    w_gu_b = w_gate_up[0].astype(BF16)
    w_down_b = w_down[0].astype(BF16)
    cos_a, sin_a = _rope_tables(dec_seq, HEAD_DIM)
    cos_d, sin_d = _rope_tables(dec_seq, B_HALF)
    conv_b3 = conv_b.reshape(depth, 1, C_CONV_CH)
    dtbias = _pad_lanes(dt_bias)
    alog = _pad_lanes(a_log)
    dskip_e = jnp.repeat(d_skip, C_HEAD_DIM, axis=-1).reshape(depth, 1, C_INNER)
    head_of_lane = jnp.arange(C_INNER) // C_HEAD_DIM
    e_f = (jnp.arange(LANES)[:, None] == head_of_lane[None, :]).astype(BF16)
    e_b = (jnp.arange(LANES)[:, None] == head_of_lane[None, :] + C_HEADS).astype(BF16)
    e3 = jnp.stack([jnp.concatenate([e_f] * 3, axis=0), jnp.concatenate([e_b] * 3, axis=0)])
    ck_a = cache_attn_k.reshape(dec_batch, depth, past, A_KV_HEADS * HEAD_DIM)
    cv_a = cache_attn_v.reshape(dec_batch, depth, past, A_KV_HEADS * HEAD_DIM)
    ck_d = cache_diff_k.reshape(dec_batch, depth, past, B_HEADS * HEAD_DIM)
    cv_d = cache_diff_v.reshape(dec_batch, depth, past, B_HEADS * HEAD_DIM)
    h0f = state_ssm_fwd.reshape(dec_batch, depth, C_INNER, C_STATE)
    h0b = state_ssm_bwd.reshape(dec_batch, depth, C_INNER, C_STATE)

    cond8 = jnp.concatenate([c_ctx[None, :], c, jnp.zeros((8 - 1 - dec_batch, d), F32)], axis=0)
    mod = _ada_call(cond8, w_ada, b_ada)[:, :1 + dec_batch].reshape(depth, 1 + dec_batch, N_MOD, d)

    h = jnp.concatenate([x_prompt.reshape(n_ctx, d), x_sample.reshape(dec_batch * dec_seq, d)], axis=0)
    ctx_out = []
    for l in range(depth):
        lam_init = _lambda_init(l)
        w_in_dt = jnp.pad(w_in_b[:, P_COLS:], ((0, 0), (0, LANES - DT_COLS)))
        p, dt, kv = _inproj_call(h, mod[l], norm_mix[l][None, :], w_in_b, w_in_dt, mod_row, n_ctx, tm_big)
        oa = _attn_a_calls(p, attn_sink, ck_a, cv_a, cos_a, sin_a, l, dims)
        has_next = l + 1 < depth
        res = _attn_b_calls(p, diff_lambda[l], diff_norm[l][None, :], ck_d, cv_d, cos_d, sin_d,
                            l, lam_init, dims, tq, (w_in, w_out, l + 1) if has_next else None)
        od = res[0]
        xact = _conv_call(p, conv_w, conv_b3, l, dims)
        yf, yb, hf, hb = _ssd_call(xact, dt, dtbias[l], alog[l], dskip_e[l], e3, h0f, h0b, l, dims)
        h = _outproj_call(oa, od, yf, yb, p, ssm_norm[l][None, :], h, mod[l], w_out_b, mod_row, tm)
        if has_next:
            w_in_b, w_out_b = res[1], res[2]
        nxt32 = (w_gate_up, w_down, l + 1) if has_next else None
        res = _ffn_call(h, mod[l], norm_ffn[l][None, :], w_gu_b, w_down_b, mod_row, tm, 512, nxt32)
        h = res[0]
        if nxt32 is not None:
            w_gu_b, w_down_b = res[1], res[2]

        def ctx_cols(lo, width, heads):
            return kv[:n_ctx, lo:lo + width].reshape(batch, seq, heads, HEAD_DIM)

        akv, bw = A_KV_HEADS * HEAD_DIM, B_HEADS * HEAD_DIM
        ctx_out.append((ctx_cols(0, akv, A_KV_HEADS), ctx_cols(akv, akv, A_KV_HEADS),
                        ctx_cols(2 * akv, bw, B_HEADS), ctx_cols(2 * akv + bw, bw, B_HEADS),
                        hf, hb))

    y_prompt, y_sample = _final_norm_call(h, norm_final[None, :], n_ctx, tm)
    stacked = [jnp.stack([t[i] for t in ctx_out], axis=1) for i in range(6)]
    return (y_prompt.reshape(batch, seq, d), y_sample.reshape(dec_batch, dec_seq, d), *stacked)
```

```python
import functools
import math

import jax
import jax.numpy as jnp
from jax import lax
from jax.experimental import pallas as pl
from jax.experimental.pallas import tpu as pltpu

F32 = jnp.float32
BF16 = jnp.bfloat16

HEAD_DIM = 128
A_HEADS = 4
A_KV_HEADS = 2
A_GROUP = A_HEADS // A_KV_HEADS
BLOCK = 128
B_HEADS = 4
B_HALF = HEAD_DIM // 2
C_HEADS = 16
C_HEAD_DIM = 64
C_INNER = C_HEADS * C_HEAD_DIM
C_GROUPS = 2
C_STATE = 128
C_CONV = 5
C_CHUNK = 128
C_CONV_CH = C_INNER + 2 * C_GROUPS * C_STATE
GRID_W = 64
EPS = 1e-6
ROPE_BASE = 10000.0
N_MOD = 6
LANES = 128
HALO = 16
NEG = -1e30
LOG2E = math.log2(math.e)

COL_QA, COL_KA, COL_VA, COL_QD, COL_KD, COL_VD, COL_Z, COL_XBC = 0, 4, 6, 8, 12, 16, 20, 28
P_COLS = 40 * LANES
HALF_INNER = C_INNER // C_GROUPS
DT_COLS = 2 * C_HEADS

VMEM_LIMIT = 48 * 1024 * 1024


def _cparams(*sem, vmem=VMEM_LIMIT):
    return pltpu.CompilerParams(dimension_semantics=sem, vmem_limit_bytes=vmem)


def _dot(a, b):
    return jnp.dot(a, b, preferred_element_type=F32)


def _dot_nt(a, b):
    return lax.dot_general(a, b, (((1,), (1,)), ((), ())), preferred_element_type=F32)


def _sigmoid(x):
    return 1.0 / (1.0 + jnp.exp(-x))


def _split3(a):
    hi = a.astype(BF16)
    r = a - hi.astype(F32)
    mid = r.astype(BF16)
    lo = (r - mid.astype(F32)).astype(BF16)
    return hi, mid, lo


def _ada_kernel(cond_ref, w_ref, b_ref, o_ref):
    s = cond_ref[...]
    s = s * _sigmoid(s)
    o_ref[...] = _dot(s.astype(BF16), w_ref[...].astype(BF16)) + b_ref[...]


def _ada_call(cond8, w_ada, b_ada, tn=1024):
    depth, d, n = w_ada.shape
    return pl.pallas_call(
        _ada_kernel,
        grid=(depth, n // tn),
        in_specs=[
            pl.BlockSpec((8, d), lambda l, j: (0, 0)),
            pl.BlockSpec((None, d, tn), lambda l, j: (l, 0, j)),
            pl.BlockSpec((None, 1, tn), lambda l, j: (l, 0, j)),
        ],
        out_specs=pl.BlockSpec((None, 8, tn), lambda l, j: (l, 0, j)),
        out_shape=jax.ShapeDtypeStruct((depth, 8, n), F32),
        compiler_params=_cparams("parallel", "parallel"),
        name="adaln",
    )(cond8, w_ada, b_ada.reshape(depth, 1, n))


def _modnorm(x, nw, shift, scale):
    ms = jnp.mean(x * x, axis=-1, keepdims=True)
    y = x * lax.rsqrt(ms + EPS) * nw
    return y * (1.0 + scale) + shift


def _inproj_kernel(h_ref, mod_ref, nw_ref, w_ref, wdt_ref, p_ref, dt_ref, kv_ref, u_sc, *, n_ctx_tiles, kv_cols):
    i = pl.program_id(0)
    j = pl.program_id(1)

    @pl.when(j == 0)
    def _():
        u = _modnorm(h_ref[...], nw_ref[...], mod_ref[0:1, :], mod_ref[1:2, :]).astype(BF16)
        u_sc[...] = u
        dt_ref[...] = _dot(u, wdt_ref[...])

    res = _dot(u_sc[...], w_ref[...])
    p_ref[...] = res.astype(p_ref.dtype)

    tn = res.shape[1]
    kvw = kv_ref.shape[1]
    for piece, col in enumerate(kv_cols):
        assert col // tn == piece

        @pl.when((i < n_ctx_tiles) & (j == piece))
        def _():
            kv_ref[...] = res[:, col % tn:col % tn + kvw]

    @pl.when((i == n_ctx_tiles) & (j < len(kv_cols)))
    def _():
        kv_ref[...] = jnp.zeros_like(kv_ref)


def _inproj_call(h, mod_l, nw, w, wdt, layer, mod_row, n_ctx, tm, tn=1024):
    t, d = h.shape
    nc = n_ctx // tm
    kvw = HALF_INNER
    kv_cols = (COL_KA * LANES, COL_KD * LANES, COL_VD * LANES)
    last = len(kv_cols) - 1
    return pl.pallas_call(
        functools.partial(_inproj_kernel, n_ctx_tiles=nc, kv_cols=kv_cols),
        grid=(t // tm, P_COLS // tn),
        in_specs=[
            pl.BlockSpec((tm, d), lambda i, j: (i, 0)),
            pl.BlockSpec((None, N_MOD, d), lambda i, j: (mod_row(i * tm), 0, 0)),
            pl.BlockSpec((1, d), lambda i, j: (0, 0)),
            pl.BlockSpec((None, d, tn), lambda i, j: (layer, 0, j)),
            pl.BlockSpec((None, d, LANES), lambda i, j: (layer, 0, 0)),
        ],
        out_specs=[
            pl.BlockSpec((tm, tn), lambda i, j: (i, j)),
            pl.BlockSpec((tm, LANES), lambda i, j: (i, 0)),
            pl.BlockSpec((tm, kvw), lambda i, j: (jnp.minimum(i, nc), jnp.where(i <= nc, jnp.minimum(j, last), last))),
        ],
        out_shape=[jax.ShapeDtypeStruct((t, P_COLS), BF16), jax.ShapeDtypeStruct((t, LANES), F32),
                   jax.ShapeDtypeStruct((n_ctx + tm, kvw * len(kv_cols)), F32)],
        scratch_shapes=[pltpu.VMEM((tm, d), BF16)],
        compiler_params=_cparams("arbitrary", "arbitrary"),
        name="inproj",
    )(h, mod_l, nw, w, wdt)


def _rope_a(x, c, s):
    x = x.astype(F32)
    return x * c + pltpu.roll(x, HEAD_DIM // 2, axis=1) * s


def _rope_d(x, c, s):
    x = x.astype(F32)
    lane = lax.broadcasted_iota(jnp.int32, x.shape, 1)
    q = B_HALF // 2
    partner = jnp.where((lane & (B_HALF - 1)) < q, pltpu.roll(x, LANES - q, axis=1), pltpu.roll(x, q, axis=1))
    return x * c + partner * s


def _sink_softmax_pv(s, sink, v):
    m = jnp.maximum(jnp.max(s, axis=-1, keepdims=True), sink)
    e = jnp.exp2(s - m)
    den = jnp.sum(e, axis=-1, keepdims=True) + jnp.exp2(sink - m)
    return _dot(e.astype(BF16), v) / den


def _attn_a_ctx_kernel(sink_ref, q_ref, k_ref, v_ref, buf_ref, o_ref, *, layer):
    del buf_ref
    scale = HEAD_DIM ** -0.5 * LOG2E
    for kv in range(A_KV_HEADS):
        ks = slice(kv * HEAD_DIM, (kv + 1) * HEAD_DIM)
        k = k_ref[:, ks].astype(BF16)
        v = v_ref[:, ks].astype(BF16)
        for g in range(A_GROUP):
            head = kv * A_GROUP + g
            sl = slice(head * HEAD_DIM, (head + 1) * HEAD_DIM)
            q = (q_ref[:, sl].astype(F32) * scale).astype(BF16)
            o = _sink_softmax_pv(_dot_nt(q, k), sink_ref[layer, head] * LOG2E, v)
            o_ref[:, sl] = o.astype(o_ref.dtype)


def _attn_a_lat_kernel(sink_ref, q_ref, kp_ref, kc_ref, kn_ref, vp_ref, vc_ref, vn_ref,
                       kx_ref, vx_ref, cq_ref, sq_ref, cp_ref, sp_ref, cn_ref, sn_ref,
                       ctx_rows_ref, o_ref, *, layer, n_steps):
    del ctx_rows_ref
    nb = pl.program_id(1)
    scale = HEAD_DIM ** -0.5 * LOG2E
    cq, sq = cq_ref[...], sq_ref[...]
    cp, sp, cn, sn = cp_ref[...], sp_ref[...], cn_ref[...], sn_ref[...]
    n_ctx = kx_ref.shape[0]
    tq = q_ref.shape[0]
    win = tq + 2 * BLOCK
    qi = lax.broadcasted_iota(jnp.int32, (tq, win + n_ctx), 0)
    kj = lax.broadcasted_iota(jnp.int32, (tq, win + n_ctx), 1)
    rel = kj - BLOCK - qi
    off_seq = ((kj < BLOCK) & (nb == 0)) | ((kj >= tq + BLOCK) & (nb == n_steps - 1))
    mask = jnp.logical_not((kj < win) & ((rel > BLOCK) | (rel < -BLOCK) | off_seq))
    for kv in range(A_KV_HEADS):
        ks = slice(kv * HEAD_DIM, (kv + 1) * HEAD_DIM)
        k_all = jnp.concatenate([
            _rope_a(kp_ref[:, ks], cp, sp).astype(BF16),
            _rope_a(kc_ref[:, ks], cq, sq).astype(BF16),
            _rope_a(kn_ref[:, ks], cn, sn).astype(BF16),
            kx_ref[:, ks].astype(BF16)], axis=0)
        v_all = jnp.concatenate([r[:, ks].astype(BF16) for r in (vp_ref, vc_ref, vn_ref, vx_ref)], axis=0)
        for g in range(A_GROUP):
            head = kv * A_GROUP + g
            sl = slice(head * HEAD_DIM, (head + 1) * HEAD_DIM)
            q = (_rope_a(q_ref[:, sl], cq, sq) * scale).astype(BF16)
            s = jnp.where(mask, _dot_nt(q, k_all), NEG)
            o = _sink_softmax_pv(s, sink_ref[layer, head] * LOG2E, v_all)
            o_ref[:, sl] = o.astype(o_ref.dtype)


def _attn_a_calls(p, sink, ck, cv, cos_a, sin_a, layer, dims):
    t = p.shape[0]
    batch, seq, dec_batch, dec_seq = dims
    n_ctx = batch * seq
    smem = pl.BlockSpec(memory_space=pltpu.SMEM)
    aw = A_HEADS * HEAD_DIM
    kw = A_KV_HEADS * HEAD_DIM
    qcol, kcol, vcol = COL_QA * LANES // aw, COL_KA * LANES // kw, COL_VA * LANES // kw
    oa_ctx = pl.pallas_call(
        functools.partial(_attn_a_ctx_kernel, layer=layer),
        grid=(batch,),
        in_specs=[
            smem,
            pl.BlockSpec((seq, aw), lambda b: (b, qcol)),
            pl.BlockSpec((seq, kw), lambda b: (b, kcol)),
            pl.BlockSpec((seq, kw), lambda b: (b, vcol)),
            pl.BlockSpec(memory_space=pl.ANY),
        ],
        out_specs=pl.BlockSpec((seq, aw), lambda b: (b, 0)),
        out_shape=jax.ShapeDtypeStruct((t, aw), BF16),
        input_output_aliases={4: 0},
        compiler_params=_cparams("parallel"),
        name="attn_a_ctx",
    )(sink, p, p, p, jnp.zeros((t, aw), BF16))

    nbl = dec_seq // BLOCK
    base = n_ctx // BLOCK
    qb = 2 if (nbl % 2 == 0 and n_ctx % (2 * BLOCK) == 0) else 1
    tq = qb * BLOCK
    steps = nbl // qb

    def run(b, n):
        return n_ctx // tq + b * steps + n

    def prev(n):
        return jnp.maximum(qb * n - 1, 0)

    def nxt(n):
        return jnp.minimum(qb * n + qb, nbl - 1)

    def blk(b, i):
        return base + b * nbl + i

    past = ck.shape[2]
    kblk = (BLOCK, kw)
    tblk = (BLOCK, HEAD_DIM)
    oa_lat = pl.pallas_call(
        functools.partial(_attn_a_lat_kernel, layer=layer, n_steps=steps),
        grid=(dec_batch, steps),
        in_specs=[
            smem,
            pl.BlockSpec((tq, aw), lambda b, n: (run(b, n), qcol)),
            pl.BlockSpec(kblk, lambda b, n: (blk(b, prev(n)), kcol)),
            pl.BlockSpec((tq, kw), lambda b, n: (run(b, n), kcol)),
            pl.BlockSpec(kblk, lambda b, n: (blk(b, nxt(n)), kcol)),
            pl.BlockSpec(kblk, lambda b, n: (blk(b, prev(n)), vcol)),
            pl.BlockSpec((tq, kw), lambda b, n: (run(b, n), vcol)),
            pl.BlockSpec(kblk, lambda b, n: (blk(b, nxt(n)), vcol)),
            pl.BlockSpec((None, None, past, kw), lambda b, n: (b, layer, 0, 0)),
            pl.BlockSpec((None, None, past, kw), lambda b, n: (b, layer, 0, 0)),
            pl.BlockSpec((tq, HEAD_DIM), lambda b, n: (n, 0)),
            pl.BlockSpec((tq, HEAD_DIM), lambda b, n: (n, 0)),
            pl.BlockSpec(tblk, lambda b, n: (prev(n), 0)),
            pl.BlockSpec(tblk, lambda b, n: (prev(n), 0)),
            pl.BlockSpec(tblk, lambda b, n: (nxt(n), 0)),
            pl.BlockSpec(tblk, lambda b, n: (nxt(n), 0)),
            pl.BlockSpec(memory_space=pl.ANY),
        ],
        out_specs=pl.BlockSpec((tq, aw), lambda b, n: (run(b, n), 0)),
        out_shape=jax.ShapeDtypeStruct((t, aw), BF16),
        input_output_aliases={16: 0},
        compiler_params=_cparams("parallel", "parallel"),
        name="attn_a_lat",
    )(sink, p, p, p, p, p, p, p, ck, cv, cos_a, sin_a, cos_a, sin_a, cos_a, sin_a, oa_ctx)
    return oa_lat


def _diff_lambda(lv, lam_init):
    a = jnp.sum(lv[0:1, :] * lv[1:2, :], axis=-1, keepdims=True)
    b = jnp.sum(lv[2:3, :] * lv[3:4, :], axis=-1, keepdims=True)
    return jnp.exp(a) - jnp.exp(b) + lam_init


def _diff_core(q, k, v, lam, dn_w, lam_init):
    lane = lax.broadcasted_iota(jnp.int32, q.shape, 1)
    qs = q.astype(F32) * (B_HALF ** -0.5 * LOG2E)
    outs = []
    for half in range(2):
        sel = (lane < B_HALF) if half == 0 else (lane >= B_HALF)
        s = _dot_nt(jnp.where(sel, qs, 0.0).astype(BF16), k)
        e = jnp.exp2(s - jnp.max(s, axis=-1, keepdims=True))
        den = jnp.sum(e, axis=-1, keepdims=True)
        outs.append(_dot(e.astype(BF16), v) / den)
    o = outs[0] - lam * outs[1]
    ms = jnp.mean(o * o, axis=-1, keepdims=True)
    return o * lax.rsqrt(ms + EPS) * dn_w * (1.0 - lam_init)


def _attn_b_ctx_kernel(lv_ref, dn_ref, q_ref, k_ref, v_ref, buf_ref, o_ref, *, lam_init):
    del buf_ref
    lam = _diff_lambda(lv_ref[...], lam_init)
    for head in range(B_HEADS):
        sl = slice(head * HEAD_DIM, (head + 1) * HEAD_DIM)
        o = _diff_core(q_ref[:, sl], k_ref[:, sl].astype(BF16), v_ref[:, sl].astype(BF16), lam, dn_ref[...],
                       lam_init)
        o_ref[:, sl] = o.astype(o_ref.dtype)


def _attn_b_lat_kernel(lv_ref, dn_ref, q_ref, k_ref, v_ref, kx_ref, vx_ref, cq_ref, sq_ref,
                       ck_ref, sk_ref, ctx_rows_ref, o_ref, k_sc, v_sc, *, lam_init):
    del ctx_rows_ref
    n = k_ref.shape[0]

    @pl.when(pl.program_id(2) == 0)
    def _():
        k_sc[0:n, :] = _rope_d(k_ref[...], ck_ref[...], sk_ref[...]).astype(BF16)
        k_sc[n:, :] = kx_ref[...].astype(BF16)
        v_sc[0:n, :] = v_ref[...].astype(BF16)
        v_sc[n:, :] = vx_ref[...].astype(BF16)

    lam = _diff_lambda(lv_ref[...], lam_init)
    q = _rope_d(q_ref[...], cq_ref[...], sq_ref[...])
    o = _diff_core(q, k_sc[...], v_sc[...], lam, dn_ref[...], lam_init)
    o_ref[...] = o.astype(o_ref.dtype)


def _attn_b_calls(p, lv, dn_w, ck, cv, cos_d, sin_d, layer, lam_init, dims, tq):
    batch, seq, dec_batch, dec_seq = dims
    n_ctx = batch * seq
    hd = HEAD_DIM
    bw = B_HEADS * hd
    od_ctx = pl.pallas_call(
        functools.partial(_attn_b_ctx_kernel, lam_init=lam_init),
        grid=(batch,),
        in_specs=[
            pl.BlockSpec((4, B_HALF), lambda b: (0, 0)),
            pl.BlockSpec((1, hd), lambda b: (0, 0)),
            pl.BlockSpec((seq, bw), lambda b: (b, COL_QD * LANES // bw)),
            pl.BlockSpec((seq, bw), lambda b: (b, COL_KD * LANES // bw)),
            pl.BlockSpec((seq, bw), lambda b: (b, COL_VD * LANES // bw)),
            pl.BlockSpec(memory_space=pl.ANY),
        ],
        out_specs=pl.BlockSpec((seq, bw), lambda b: (b, 0)),
        out_shape=jax.ShapeDtypeStruct((p.shape[0], bw), BF16),
        input_output_aliases={5: 0},
        compiler_params=_cparams("parallel"),
        name="attn_b_ctx",
    )(lv, dn_w, p, p, p, jnp.zeros((p.shape[0], bw), BF16))

    past = ck.shape[2]
    nq = dec_seq // tq
    qbase = n_ctx // tq
    sbase = n_ctx // dec_seq
    od_lat = pl.pallas_call(
        functools.partial(_attn_b_lat_kernel, lam_init=lam_init),
        grid=(dec_batch, B_HEADS, nq),
        in_specs=[
            pl.BlockSpec((4, B_HALF), lambda b, h, i: (0, 0)),
            pl.BlockSpec((1, hd), lambda b, h, i: (0, 0)),
            pl.BlockSpec((tq, hd), lambda b, h, i: (qbase + b * nq + i, COL_QD + h)),
            pl.BlockSpec((dec_seq, hd), lambda b, h, i: (sbase + b, COL_KD + h)),
            pl.BlockSpec((dec_seq, hd), lambda b, h, i: (sbase + b, COL_VD + h)),
            pl.BlockSpec((None, None, past, hd), lambda b, h, i: (b, layer, 0, h)),
            pl.BlockSpec((None, None, past, hd), lambda b, h, i: (b, layer, 0, h)),
            pl.BlockSpec((tq, hd), lambda b, h, i: (i, 0)),
            pl.BlockSpec((tq, hd), lambda b, h, i: (i, 0)),
            pl.BlockSpec((dec_seq, hd), lambda b, h, i: (0, 0)),
            pl.BlockSpec((dec_seq, hd), lambda b, h, i: (0, 0)),
            pl.BlockSpec(memory_space=pl.ANY),
        ],
        out_specs=pl.BlockSpec((tq, hd), lambda b, h, i: (qbase + b * nq + i, h)),
        out_shape=jax.ShapeDtypeStruct((p.shape[0], B_HEADS * hd), BF16),
        input_output_aliases={11: 0},
        scratch_shapes=[pltpu.VMEM((dec_seq + past, hd), BF16), pltpu.VMEM((dec_seq + past, hd), BF16)],
        compiler_params=_cparams("parallel", "parallel", "arbitrary"),
        name="attn_b_lat",
    )(lv, dn_w, p, p, p, ck, cv, cos_d, sin_d, cos_d, sin_d, od_ctx)
    return od_lat


def _expand_heads(a, e3):
    return _dot(jnp.concatenate(_split3(a), axis=1), e3)


def _ssd_direction(xact, dt, a_row, e3, h_sc, hoff, fwd, dskip):
    L = C_CHUNK
    li = lax.broadcasted_iota(jnp.int32, (L, L), 0)
    si = lax.broadcasted_iota(jnp.int32, (L, L), 1)
    tri = (li >= si) if fwd else (li <= si)
    tri_b = jnp.where(tri, 1.0, 0.0).astype(BF16)
    dta = dt * a_row
    cum = _dot(jnp.concatenate([tri_b, tri_b, tri_b], axis=1), jnp.concatenate(_split3(dta), axis=0))
    cum_t = cum.T
    dt_t = dt.T
    end = cum[L - 1:L, :] if fwd else cum[0:1, :]
    to_end = jnp.exp(end - cum) * dt
    ecum = jnp.exp(cum)
    x = xact[:, :C_INNER]
    x_te = (x * _expand_heads(to_end, e3)).astype(BF16)
    dec_row = _expand_heads(jnp.broadcast_to(jnp.exp(end), (8, LANES)), e3)[0:1, :]
    lane = lax.broadcasted_iota(jnp.int32, (L, LANES), 1)
    lo = lane < C_HEAD_DIM
    gw = C_INNER // C_GROUPS
    ys, offs = [], []
    for g in range(C_GROUPS):
        bm = xact[:, C_INNER + g * C_STATE:C_INNER + (g + 1) * C_STATE]
        cm = xact[:, C_INNER + (C_GROUPS + g) * C_STATE:C_INNER + (C_GROUPS + g + 1) * C_STATE]
        cmb = cm.astype(BF16)
        cb = _dot_nt(cmb, bm.astype(BF16))
        h_prev = h_sc[:, g * gw:(g + 1) * gw]
        offs.append(_dot(cmb, h_prev.astype(BF16)))
        for pair in range(gw // LANES):
            blk = g * (gw // LANES) + pair
            w_parts = []
            for j in range(2):
                c = hoff + 2 * blk + j
                seg = cum[:, c:c + 1] - cum_t[c:c + 1, :]
                w = cb * jnp.exp(jnp.where(tri, seg, NEG)) * dt_t[c:c + 1, :]
                w_parts.append(w.astype(BF16))
            xb = x[:, blk * LANES:(blk + 1) * LANES]
            rhs = jnp.concatenate([jnp.where(lo, xb, 0.0), jnp.where(lo, 0.0, xb)], axis=0).astype(BF16)
            ys.append(_dot(jnp.concatenate(w_parts, axis=1), rhs))
        st = _dot(bm.T.astype(BF16), x_te[:, g * gw:(g + 1) * gw])
        h_sc[:, g * gw:(g + 1) * gw] = dec_row[:, g * gw:(g + 1) * gw] * h_prev + st
    y = jnp.concatenate(ys, axis=1) + jnp.concatenate(offs, axis=1) * _expand_heads(ecum, e3)
    if dskip is not None:
        y = y + dskip * x
    return y


def _conv_kernel(*refs, ncol, n_ctx_tiles, ctx_tps, lat_tps):
    x_refs, xp_refs, xn_refs = refs[:ncol], refs[ncol:2 * ncol], refs[2 * ncol:3 * ncol]
    cw_ref, cbias_ref, o_ref = refs[3 * ncol:]
    t = pl.program_id(0)
    is_ctx = t < n_ctx_tiles
    pos = jnp.where(is_ctx, t % ctx_tps, (t - n_ctx_tiles) % lat_tps)
    tps = jnp.where(is_ctx, ctx_tps, lat_tps)
    rows = o_ref.shape[0]

    def cols(rs):
        return jnp.concatenate([r[...].astype(F32) for r in rs], axis=1)

    xp = jnp.concatenate([jnp.where(pos > 0, cols(xp_refs), 0.0), cols(x_refs),
                          jnp.where(pos < tps - 1, cols(xn_refs), 0.0)], axis=0)
    n = rows + 2 * HALO
    acc = jnp.broadcast_to(cbias_ref[...], o_ref.shape)
    for k in range(C_CONV):
        shifted = xp if k == C_CONV // 2 else pltpu.roll(xp, (C_CONV // 2 - k) % n, axis=0)
        acc = acc + shifted[HALO:HALO + rows] * cw_ref[k:k + 1, :]
    o_ref[...] = acc * _sigmoid(acc)


def _conv_call(p, conv_w, conv_b, layer, dims, rows=512):
    batch, seq, dec_batch, dec_seq = dims
    t = p.shape[0]
    rows = min(rows, seq)
    assert seq % rows == 0 and dec_seq % rows == 0
    hpt = rows // HALO
    last_halo = t // HALO - 1
    cw = HALF_INNER
    c0 = COL_XBC * LANES // cw
    ncol = C_CONV_CH // cw
    main = [pl.BlockSpec((rows, cw), lambda i, c=c: (i, c0 + c)) for c in range(ncol)]
    prev = [pl.BlockSpec((HALO, cw), lambda i, c=c: (jnp.maximum(i * hpt - 1, 0), c0 + c)) for c in range(ncol)]
    nxt = [pl.BlockSpec((HALO, cw), lambda i, c=c: (jnp.minimum((i + 1) * hpt, last_halo), c0 + c))
           for c in range(ncol)]
    return pl.pallas_call(
        functools.partial(_conv_kernel, ncol=ncol, n_ctx_tiles=batch * seq // rows, ctx_tps=seq // rows,
                          lat_tps=dec_seq // rows),
        grid=(t // rows,),
        in_specs=main + prev + nxt + [
            pl.BlockSpec((None, C_CONV, C_CONV_CH), lambda i: (layer, 0, 0)),
            pl.BlockSpec((None, 1, C_CONV_CH), lambda i: (layer, 0, 0)),
        ],
        out_specs=pl.BlockSpec((rows, C_CONV_CH), lambda i: (i, 0)),
        out_shape=jax.ShapeDtypeStruct((t, C_CONV_CH), F32),
        compiler_params=_cparams("parallel"),
        name="conv",
    )(*([p] * (3 * ncol)), conv_w, conv_b)


def _ssd_kernel(xf_ref, xb_ref, dtf_ref, dtb_ref,
                dtbias_ref, alog_ref, dskip_ref, e3_ref, h0f_ref, h0b_ref,
                yf_ref, yb_ref, hfo_ref, hbo_ref, hf_sc, hb_sc,
                *, n_ctx_chunks, ctx_cps, lat_cps):
    s = pl.program_id(0)
    is_ctx = s < n_ctx_chunks
    pos = jnp.where(is_ctx, s % ctx_cps, (s - n_ctx_chunks) % lat_cps)
    cps = jnp.where(is_ctx, ctx_cps, lat_cps)
    first = pos == 0
    last = pos == cps - 1

    @pl.when(first & is_ctx)
    def _():
        hf_sc[...] = jnp.zeros_like(hf_sc)
        hb_sc[...] = jnp.zeros_like(hb_sc)

    @pl.when(first & jnp.logical_not(is_ctx))
    def _():
        hf_sc[...] = h0f_ref[...].T
        hb_sc[...] = h0b_ref[...].T

    def softplus(v):
        return jnp.maximum(v, 0.0) + jnp.log(1.0 + jnp.exp(-jnp.abs(v)))

    a_row = -jnp.exp(alog_ref[...])
    e3 = e3_ref[...]

    group = xf_ref.shape[0] // C_CHUNK
    for c in range(group):
        rows = slice(c * C_CHUNK, (c + 1) * C_CHUNK)
        dt = softplus(dtf_ref[rows, :] + dtbias_ref[...])
        yf = _ssd_direction(xf_ref[rows, :], dt, a_row, e3[0], hf_sc, 0, True, dskip_ref[...])
        yf_ref[rows, :] = yf.astype(yf_ref.dtype)
    for c in reversed(range(group)):
        rows = slice(c * C_CHUNK, (c + 1) * C_CHUNK)
        dt = softplus(dtb_ref[rows, :] + dtbias_ref[...])
        yb = _ssd_direction(xb_ref[rows, :], dt, a_row, e3[1], hb_sc, C_HEADS, False, None)
        yb_ref[rows, :] = yb.astype(yb_ref.dtype)

    @pl.when(last & is_ctx)
    def _():
        hfo_ref[...] = hf_sc[...].T.reshape(hfo_ref.shape)
        hbo_ref[...] = hb_sc[...].T.reshape(hbo_ref.shape)


def _ssd_call(xact, dt, dtbias, alog, dskip_e, e3, h0f, h0b, layer, dims):
    batch, seq, dec_batch, dec_seq = dims
    group = 2 if (seq % (2 * C_CHUNK) == 0 and dec_seq % (2 * C_CHUNK) == 0) else 1
    L = group * C_CHUNK
    ctx_cps, lat_cps = seq // L, dec_seq // L
    n_ctx_chunks = batch * ctx_cps
    n_chunks = n_ctx_chunks + dec_batch * lat_cps
    t = xact.shape[0]

    def mirror(s):
        c_ctx = (s // ctx_cps) * ctx_cps + (ctx_cps - 1 - s % ctx_cps)
        r = s - n_ctx_chunks
        c_lat = n_ctx_chunks + (r // lat_cps) * lat_cps + (lat_cps - 1 - r % lat_cps)
        return jnp.where(s < n_ctx_chunks, c_ctx, c_lat)

    def lat_b(s):
        return jnp.maximum(s - n_ctx_chunks, 0) // lat_cps

    def ctx_b(s):
        return jnp.minimum(s // ctx_cps, batch - 1)

    const2 = lambda s: (0, 0)
    hspec = pl.BlockSpec((None, None, C_INNER, C_STATE), lambda s: (lat_b(s), layer, 0, 0))
    ospec = pl.BlockSpec((None, C_HEADS, C_HEAD_DIM, C_STATE), lambda s: (ctx_b(s), 0, 0, 0))
    return pl.pallas_call(
        functools.partial(_ssd_kernel, n_ctx_chunks=n_ctx_chunks, ctx_cps=ctx_cps, lat_cps=lat_cps),
        grid=(n_chunks,),
        in_specs=[
            pl.BlockSpec((L, C_CONV_CH), lambda s: (s, 0)),
            pl.BlockSpec((L, C_CONV_CH), lambda s: (mirror(s), 0)),
            pl.BlockSpec((L, LANES), lambda s: (s, 0)),
            pl.BlockSpec((L, LANES), lambda s: (mirror(s), 0)),
            pl.BlockSpec((1, LANES), const2),
            pl.BlockSpec((1, LANES), const2),
            pl.BlockSpec((1, C_INNER), const2),
            pl.BlockSpec((2, 3 * LANES, C_INNER), lambda s: (0, 0, 0)),
            hspec, hspec,
        ],
        out_specs=[
            pl.BlockSpec((L, C_INNER), lambda s: (s, 0)),
            pl.BlockSpec((L, C_INNER), lambda s: (mirror(s), 0)),
            ospec, ospec,
        ],
        out_shape=[
            jax.ShapeDtypeStruct((t, C_INNER), BF16), jax.ShapeDtypeStruct((t, C_INNER), BF16),
            jax.ShapeDtypeStruct((batch, C_HEADS, C_HEAD_DIM, C_STATE), F32),
            jax.ShapeDtypeStruct((batch, C_HEADS, C_HEAD_DIM, C_STATE), F32),
        ],
        scratch_shapes=[
            pltpu.VMEM((C_STATE, C_INNER), F32),
            pltpu.VMEM((C_STATE, C_INNER), F32),
        ],
        compiler_params=_cparams("arbitrary"),
        name="ssd",
    )(xact, xact, dt, dt, dtbias, alog, dskip_e, e3, h0f, h0b)


def _outproj_kernel(oa_ref, od_ref, yf_ref, yb_ref, z0_ref, z1_ref, sn_ref, h_ref, mod_ref, w_ref, o_ref):
    na = oa_ref.shape[1]
    nd = od_ref.shape[1]
    acc = _dot(oa_ref[...], w_ref[0:na, :]) + _dot(od_ref[...], w_ref[na:na + nd, :])
    gw = HALF_INNER
    for g, z_ref in enumerate((z0_ref, z1_ref)):
        z = z_ref[...].astype(F32)
        y = yf_ref[:, g * gw:(g + 1) * gw].astype(F32) + yb_ref[:, g * gw:(g + 1) * gw].astype(F32)
        yg = y * (z * _sigmoid(z))
        yg = yg * lax.rsqrt(jnp.mean(yg * yg, axis=-1, keepdims=True) + EPS)
        yg = yg * sn_ref[:, g * gw:(g + 1) * gw]
        lo = na + nd + g * gw
        acc = acc + _dot(yg.astype(BF16), w_ref[lo:lo + gw, :])
    o_ref[...] = h_ref[...] + mod_ref[2:3, :] * acc


def _outproj_call(oa, od, yf, yb, p, ssm_norm, h, mod_l, w, layer, mod_row, tm):
    t, d = h.shape
    mw = w.shape[1]
    z0 = COL_Z * LANES // HALF_INNER
    return pl.pallas_call(
        _outproj_kernel,
        grid=(t // tm,),
        in_specs=[
            pl.BlockSpec((tm, oa.shape[1]), lambda i: (i, 0)),
            pl.BlockSpec((tm, od.shape[1]), lambda i: (i, 0)),
            pl.BlockSpec((tm, C_INNER), lambda i: (i, 0)),
            pl.BlockSpec((tm, C_INNER), lambda i: (i, 0)),
            pl.BlockSpec((tm, HALF_INNER), lambda i: (i, z0)),
            pl.BlockSpec((tm, HALF_INNER), lambda i: (i, z0 + 1)),
            pl.BlockSpec((1, C_INNER), lambda i: (0, 0)),
            pl.BlockSpec((tm, d), lambda i: (i, 0)),
            pl.BlockSpec((None, N_MOD, d), lambda i: (mod_row(i * tm), 0, 0)),
            pl.BlockSpec((None, mw, d), lambda i: (layer, 0, 0), pipeline_mode=pl.Buffered(1)),
        ],
        out_specs=pl.BlockSpec((tm, d), lambda i: (i, 0)),
        out_shape=jax.ShapeDtypeStruct((t, d), F32),
        compiler_params=_cparams("parallel"),
        name="outproj",
    )(oa, od, yf, yb, p, p, ssm_norm, h, mod_l, w)


def _ffn_kernel(h_ref, mod_ref, nw_ref, wg_ref, wu_ref, wd_ref, *rest, n_ctx_tiles=None):
    if n_ctx_tiles is None:
        o_ref, u_sc = rest
        _ffn_body(h_ref, mod_ref, nw_ref, wg_ref, wu_ref, wd_ref, o_ref, u_sc, None)
        return
    fw_ref, oc_ref, ol_ref, u_sc = rest
    i = pl.program_id(0)
    for cond, o_ref in ((i < n_ctx_tiles, oc_ref), (i >= n_ctx_tiles, ol_ref)):
        pl.when(cond)(functools.partial(_ffn_body, h_ref, mod_ref, nw_ref, wg_ref, wu_ref, wd_ref, o_ref, u_sc,
                                        fw_ref))


def _ffn_body(h_ref, mod_ref, nw_ref, wg_ref, wu_ref, wd_ref, o_ref, u_sc, fw_ref):
    f = pl.program_id(1)

    @pl.when(f == 0)
    def _():
        u_sc[...] = _modnorm(h_ref[...], nw_ref[...], mod_ref[3:4, :], mod_ref[4:5, :]).astype(BF16)
        o_ref[...] = jnp.zeros_like(o_ref)

    u = u_sc[...]
    half = wg_ref.shape[1] // 2
    part = None
    for lo in (0, half):
        g = _dot(u, wg_ref[:, lo:lo + half])
        a = (g * _sigmoid(g)) * _dot(u, wu_ref[:, lo:lo + half])
        d_half = _dot(a.astype(BF16), wd_ref[lo:lo + half, :])
        part = d_half if part is None else part + d_half
    o_ref[...] += part

    @pl.when(f == pl.num_programs(1) - 1)
    def _():
        y = h_ref[...] + mod_ref[5:6, :] * o_ref[...]
        if fw_ref is not None:
            y = y * lax.rsqrt(jnp.mean(y * y, axis=-1, keepdims=True) + EPS) * fw_ref[...]
        o_ref[...] = y


def _ffn_call(h, mod_l, nw, w_gu, w_d, layer, mod_row, tm, tf, final=None):
    t, d = h.shape
    ff = w_d.shape[1]
    nf = ff // tf
    in_specs = [
        pl.BlockSpec((tm, d), lambda i, f: (i, 0)),
        pl.BlockSpec((None, N_MOD, d), lambda i, f: (mod_row(i * tm), 0, 0)),
        pl.BlockSpec((1, d), lambda i, f: (0, 0)),
        pl.BlockSpec((None, d, tf), lambda i, f: (layer, 0, f)),
        pl.BlockSpec((None, d, tf), lambda i, f: (layer, 0, nf + f)),
        pl.BlockSpec((None, tf, d), lambda i, f: (layer, f, 0)),
    ]
    args = [h, mod_l, nw, w_gu, w_gu, w_d]
    if final is None:
        kern = _ffn_kernel
        out_specs = pl.BlockSpec((tm, d), lambda i, f: (i, 0))
        out_shape = jax.ShapeDtypeStruct((t, d), F32)
        sem = ("parallel", "arbitrary")
    else:
        fw, n_ctx = final
        nc = n_ctx // tm
        kern = functools.partial(_ffn_kernel, n_ctx_tiles=nc)
        in_specs.append(pl.BlockSpec((1, d), lambda i, f: (0, 0)))
        args.append(fw)
        out_specs = [pl.BlockSpec((tm, d), lambda i, f: (jnp.minimum(i, nc - 1), 0)),
                     pl.BlockSpec((tm, d), lambda i, f: (jnp.maximum(i - nc, 0), 0))]
        out_shape = [jax.ShapeDtypeStruct((n_ctx, d), F32), jax.ShapeDtypeStruct((t - n_ctx, d), F32)]
        sem = ("arbitrary", "arbitrary")
    return pl.pallas_call(
        kern,
        grid=(t // tm, nf),
        in_specs=in_specs,
        out_specs=out_specs,
        out_shape=out_shape,
        scratch_shapes=[pltpu.VMEM((tm, d), BF16)],
        compiler_params=_cparams(*sem),
        name="ffn",
    )(*args)


def _final_norm_kernel(h_ref, w_ref, oc_ref, ol_ref, *, n_ctx_tiles):
    x = h_ref[...]
    y = x * lax.rsqrt(jnp.mean(x * x, axis=-1, keepdims=True) + EPS) * w_ref[...]
    i = pl.program_id(0)

    @pl.when(i < n_ctx_tiles)
    def _():
        oc_ref[...] = y

    @pl.when(i >= n_ctx_tiles)
    def _():
        ol_ref[...] = y


def _final_norm_call(h, w, n_ctx, tm):
    t, d = h.shape
    nc = n_ctx // tm
    return pl.pallas_call(
        functools.partial(_final_norm_kernel, n_ctx_tiles=nc),
        grid=(t // tm,),
        in_specs=[pl.BlockSpec((tm, d), lambda i: (i, 0)), pl.BlockSpec((1, d), lambda i: (0, 0))],
        out_specs=[pl.BlockSpec((tm, d), lambda i: (jnp.minimum(i, nc - 1), 0)),
                   pl.BlockSpec((tm, d), lambda i: (jnp.maximum(i - nc, 0), 0))],
        out_shape=[jax.ShapeDtypeStruct((n_ctx, d), F32), jax.ShapeDtypeStruct((t - n_ctx, d), F32)],
        compiler_params=_cparams("arbitrary"),
        name="final_norm",
    )(h, w)


def _rope_tables(n, rot_dim):
    rows = n // GRID_W
    row = jnp.repeat(jnp.arange(rows), GRID_W).astype(F32)
    col = (jnp.arange(rows * GRID_W) % GRID_W).astype(F32)
    quarter = rot_dim // 4
    inv = ROPE_BASE ** (-jnp.arange(quarter, dtype=F32) / quarter)
    ang = jnp.concatenate([row[:, None] * inv, col[:, None] * inv], axis=-1)
    c, s = jnp.cos(ang), jnp.sin(ang)
    reps = LANES // rot_dim
    return (jnp.tile(jnp.concatenate([c, c], axis=-1), (1, reps)),
            jnp.tile(jnp.concatenate([-s, s], axis=-1), (1, reps)))


def _lambda_init(layer):
    return 0.8 - 0.6 * math.exp(-0.3 * layer)


def _pad_lanes(v):
    return jnp.pad(v.reshape(v.shape[0], 1, -1), ((0, 0), (0, 0), (0, LANES - DT_COLS)))


def kernel(x_prompt, x_sample, cache_attn_k, cache_attn_v, cache_diff_k, cache_diff_v, state_ssm_fwd, state_ssm_bwd, c, c_ctx, w_ada, b_ada, norm_mix, norm_ffn, w_in, attn_sink, diff_lambda, diff_norm, conv_w, conv_b, dt_bias, a_log, d_skip, ssm_norm, w_out, w_gate_up, w_down, norm_final):
    batch, seq, d = x_prompt.shape
    dec_batch, dec_seq, _ = x_sample.shape
    depth = w_in.shape[0]
    past = cache_attn_k.shape[2]
    n_ctx = batch * seq
    dims = (batch, seq, dec_batch, dec_seq)
    tm = 512 if (n_ctx % 512 == 0 and dec_seq % 512 == 0) else 256
    tm_big = 1024 if (n_ctx % 1024 == 0 and dec_seq % 1024 == 0) else tm
    tq = min(256, dec_seq)
    assert n_ctx % dec_seq == 0 and n_ctx % tm == 0 and dec_seq % tm == 0
    assert seq % C_CHUNK == 0 and dec_seq % C_CHUNK == 0 and dec_seq % GRID_W == 0
    assert 1 + dec_batch <= 8

    def mod_row(start):
        return jnp.where(start < n_ctx, 0, 1 + (start - n_ctx) // dec_seq)

    w_in_b = w_in.astype(BF16)
    w_in_dt = jnp.pad(w_in_b[:, :, P_COLS:], ((0, 0), (0, 0), (0, LANES - DT_COLS)))
    w_out_b = w_out.astype(BF16)
    w_gu_b = w_gate_up.astype(BF16)
    w_down_b = w_down.astype(BF16)
    cos_a, sin_a = _rope_tables(dec_seq, HEAD_DIM)
    cos_d, sin_d = _rope_tables(dec_seq, B_HALF)
    conv_b3 = conv_b.reshape(depth, 1, C_CONV_CH)
    dtbias = _pad_lanes(dt_bias)
    alog = _pad_lanes(a_log)
    dskip_e = jnp.repeat(d_skip, C_HEAD_DIM, axis=-1).reshape(depth, 1, C_INNER)
    head_of_lane = jnp.arange(C_INNER) // C_HEAD_DIM
    e_f = (jnp.arange(LANES)[:, None] == head_of_lane[None, :]).astype(BF16)
    e_b = (jnp.arange(LANES)[:, None] == head_of_lane[None, :] + C_HEADS).astype(BF16)
    e3 = jnp.stack([jnp.concatenate([e_f] * 3, axis=0), jnp.concatenate([e_b] * 3, axis=0)])
    ck_a = cache_attn_k.reshape(dec_batch, depth, past, A_KV_HEADS * HEAD_DIM)
    cv_a = cache_attn_v.reshape(dec_batch, depth, past, A_KV_HEADS * HEAD_DIM)
    ck_d = cache_diff_k.reshape(dec_batch, depth, past, B_HEADS * HEAD_DIM)
    cv_d = cache_diff_v.reshape(dec_batch, depth, past, B_HEADS * HEAD_DIM)
    h0f = state_ssm_fwd.reshape(dec_batch, depth, C_INNER, C_STATE)
    h0b = state_ssm_bwd.reshape(dec_batch, depth, C_INNER, C_STATE)

    cond8 = jnp.concatenate([c_ctx[None, :], c, jnp.zeros((8 - 1 - dec_batch, d), F32)], axis=0)
    mod = _ada_call(cond8, w_ada, b_ada)[:, :1 + dec_batch].reshape(depth, 1 + dec_batch, N_MOD, d)

    h = jnp.concatenate([x_prompt.reshape(n_ctx, d), x_sample.reshape(dec_batch * dec_seq, d)], axis=0)
    ctx_out = []
    for l in range(depth):
        lam_init = _lambda_init(l)
        p, dt, kv = _inproj_call(h, mod[l], norm_mix[l][None, :], w_in_b, w_in_dt, l, mod_row, n_ctx, tm_big)
        oa = _attn_a_calls(p, attn_sink, ck_a, cv_a, cos_a, sin_a, l, dims)
        od = _attn_b_calls(p, diff_lambda[l], diff_norm[l][None, :], ck_d, cv_d, cos_d, sin_d,
                           l, lam_init, dims, tq)
        xact = _conv_call(p, conv_w, conv_b3, l, dims)
        yf, yb, hf, hb = _ssd_call(xact, dt, dtbias[l], alog[l], dskip_e[l], e3, h0f, h0b, l, dims)
        h = _outproj_call(oa, od, yf, yb, p, ssm_norm[l][None, :], h, mod[l], w_out_b, l, mod_row, tm)
        final = (norm_final[None, :], n_ctx) if l == depth - 1 else None
        h = _ffn_call(h, mod[l], norm_ffn[l][None, :], w_gu_b, w_down_b, l, mod_row, tm, 512, final)

        def ctx_cols(lo, width, heads):
            return kv[:n_ctx, lo:lo + width].reshape(batch, seq, heads, HEAD_DIM)

        akv, bw = A_KV_HEADS * HEAD_DIM, B_HEADS * HEAD_DIM
        ctx_out.append((ctx_cols(0, akv, A_KV_HEADS), ctx_cols(akv, akv, A_KV_HEADS),
                        ctx_cols(2 * akv, bw, B_HEADS), ctx_cols(2 * akv + bw, bw, B_HEADS),
                        hf, hb))

    y_prompt, y_sample = h
    stacked = [jnp.stack([t[i] for t in ctx_out], axis=1) for i in range(6)]
    return (y_prompt.reshape(batch, seq, d), y_sample.reshape(dec_batch, dec_seq, d), *stacked)
```

```python
import functools
import math

import jax
import jax.numpy as jnp
from jax import lax
from jax.experimental import pallas as pl
from jax.experimental.pallas import tpu as pltpu

F32 = jnp.float32
BF16 = jnp.bfloat16

HEAD_DIM = 128
A_HEADS = 4
A_KV_HEADS = 2
A_GROUP = A_HEADS // A_KV_HEADS
BLOCK = 128
B_HEADS = 4
B_HALF = HEAD_DIM // 2
C_HEADS = 16
C_HEAD_DIM = 64
C_INNER = C_HEADS * C_HEAD_DIM
C_GROUPS = 2
C_STATE = 128
C_CONV = 5
C_CHUNK = 128
C_CONV_CH = C_INNER + 2 * C_GROUPS * C_STATE
GRID_W = 64
EPS = 1e-6
ROPE_BASE = 10000.0
N_MOD = 6
LANES = 128
HALO = 16
NEG = -1e30
LOG2E = math.log2(math.e)

COL_QA, COL_KA, COL_VA, COL_QD, COL_KD, COL_VD, COL_Z, COL_XBC = 0, 4, 6, 8, 12, 16, 20, 28
P_COLS = 40 * LANES
HALF_INNER = C_INNER // C_GROUPS
DT_COLS = 2 * C_HEADS

VMEM_LIMIT = 48 * 1024 * 1024


def _cparams(*sem, vmem=VMEM_LIMIT):
    return pltpu.CompilerParams(dimension_semantics=sem, vmem_limit_bytes=vmem)


def _dot(a, b):
    return jnp.dot(a, b, preferred_element_type=F32)


def _dot_nt(a, b):
    return lax.dot_general(a, b, (((1,), (1,)), ((), ())), preferred_element_type=F32)


def _sigmoid(x):
    return 1.0 / (1.0 + jnp.exp(-x))


def _split3(a):
    hi = a.astype(BF16)
    r = a - hi.astype(F32)
    mid = r.astype(BF16)
    lo = (r - mid.astype(F32)).astype(BF16)
    return hi, mid, lo


def _ada_kernel(cond_ref, w_ref, b_ref, o_ref):
    s = cond_ref[...]
    s = s * _sigmoid(s)
    o_ref[...] = _dot(s.astype(BF16), w_ref[...].astype(BF16)) + b_ref[...]


def _ada_call(cond8, w_ada, b_ada, tn=1024):
    depth, d, n = w_ada.shape
    return pl.pallas_call(
        _ada_kernel,
        grid=(depth, n // tn),
        in_specs=[
            pl.BlockSpec((8, d), lambda l, j: (0, 0)),
            pl.BlockSpec((None, d, tn), lambda l, j: (l, 0, j)),
            pl.BlockSpec((None, 1, tn), lambda l, j: (l, 0, j)),
        ],
        out_specs=pl.BlockSpec((None, 8, tn), lambda l, j: (l, 0, j)),
        out_shape=jax.ShapeDtypeStruct((depth, 8, n), F32),
        compiler_params=_cparams("parallel", "parallel"),
        name="adaln",
    )(cond8, w_ada, b_ada.reshape(depth, 1, n))


def _modnorm(x, nw, shift, scale):
    ms = jnp.mean(x * x, axis=-1, keepdims=True)
    y = x * lax.rsqrt(ms + EPS) * nw
    return y * (1.0 + scale) + shift


def _inproj_kernel(h_ref, mod_ref, nw_ref, w_ref, wdt_ref, p_ref, dt_ref, kv_ref, u_sc, *, n_ctx_tiles, kv_cols):
    i = pl.program_id(0)
    j = pl.program_id(1)

    @pl.when(j == 0)
    def _():
        u = _modnorm(h_ref[...], nw_ref[...], mod_ref[0:1, :], mod_ref[1:2, :]).astype(BF16)
        u_sc[...] = u
        dt_ref[...] = _dot(u, wdt_ref[...])

    res = _dot(u_sc[...], w_ref[...])
    p_ref[...] = res.astype(p_ref.dtype)

    tn = res.shape[1]
    kvw = kv_ref.shape[1]
    for piece, col in enumerate(kv_cols):
        assert col // tn == piece

        @pl.when((i < n_ctx_tiles) & (j == piece))
        def _():
            kv_ref[...] = res[:, col % tn:col % tn + kvw]

    @pl.when((i == n_ctx_tiles) & (j < len(kv_cols)))
    def _():
        kv_ref[...] = jnp.zeros_like(kv_ref)


def _inproj_call(h, mod_l, nw, w, wdt, layer, mod_row, n_ctx, tm, tn=1024):
    t, d = h.shape
    nc = n_ctx // tm
    kvw = HALF_INNER
    kv_cols = (COL_KA * LANES, COL_KD * LANES, COL_VD * LANES)
    last = len(kv_cols) - 1
    return pl.pallas_call(
        functools.partial(_inproj_kernel, n_ctx_tiles=nc, kv_cols=kv_cols),
        grid=(t // tm, P_COLS // tn),
        in_specs=[
            pl.BlockSpec((tm, d), lambda i, j: (i, 0)),
            pl.BlockSpec((None, N_MOD, d), lambda i, j: (mod_row(i * tm), 0, 0)),
            pl.BlockSpec((1, d), lambda i, j: (0, 0)),
            pl.BlockSpec((None, d, tn), lambda i, j: (layer, 0, j)),
            pl.BlockSpec((None, d, LANES), lambda i, j: (layer, 0, 0)),
        ],
        out_specs=[
            pl.BlockSpec((tm, tn), lambda i, j: (i, j)),
            pl.BlockSpec((tm, LANES), lambda i, j: (i, 0)),
            pl.BlockSpec((tm, kvw), lambda i, j: (jnp.minimum(i, nc), jnp.where(i <= nc, jnp.minimum(j, last), last))),
        ],
        out_shape=[jax.ShapeDtypeStruct((t, P_COLS), BF16), jax.ShapeDtypeStruct((t, LANES), F32),
                   jax.ShapeDtypeStruct((n_ctx + tm, kvw * len(kv_cols)), F32)],
        scratch_shapes=[pltpu.VMEM((tm, d), BF16)],
        compiler_params=_cparams("arbitrary", "arbitrary"),
        name="inproj",
    )(h, mod_l, nw, w, wdt)


def _rope_a(x, c, s):
    x = x.astype(F32)
    return x * c + pltpu.roll(x, HEAD_DIM // 2, axis=1) * s


def _rope_d(x, c, s):
    x = x.astype(F32)
    lane = lax.broadcasted_iota(jnp.int32, x.shape, 1)
    q = B_HALF // 2
    partner = jnp.where((lane & (B_HALF - 1)) < q, pltpu.roll(x, LANES - q, axis=1), pltpu.roll(x, q, axis=1))
    return x * c + partner * s


def _sink_softmax_pv(s, sink, v):
    m = jnp.maximum(jnp.max(s, axis=-1, keepdims=True), sink)
    e = jnp.exp2(s - m)
    den = jnp.sum(e, axis=-1, keepdims=True) + jnp.exp2(sink - m)
    return _dot(e.astype(BF16), v) / den


def _attn_a_ctx_kernel(sink_ref, q_ref, k_ref, v_ref, buf_ref, o_ref, *, layer):
    del buf_ref
    scale = HEAD_DIM ** -0.5 * LOG2E
    for kv in range(A_KV_HEADS):
        ks = slice(kv * HEAD_DIM, (kv + 1) * HEAD_DIM)
        k = k_ref[:, ks].astype(BF16)
        v = v_ref[:, ks].astype(BF16)
        for g in range(A_GROUP):
            head = kv * A_GROUP + g
            sl = slice(head * HEAD_DIM, (head + 1) * HEAD_DIM)
            q = (q_ref[:, sl].astype(F32) * scale).astype(BF16)
            o = _sink_softmax_pv(_dot_nt(q, k), sink_ref[layer, head] * LOG2E, v)
            o_ref[:, sl] = o.astype(o_ref.dtype)


def _attn_a_lat_kernel(sink_ref, q_ref, kp_ref, kc_ref, kn_ref, vp_ref, vc_ref, vn_ref,
                       kx_ref, vx_ref, cq_ref, sq_ref, cp_ref, sp_ref, cn_ref, sn_ref,
                       ctx_rows_ref, o_ref, *, layer, n_steps):
    del ctx_rows_ref
    nb = pl.program_id(1)
    scale = HEAD_DIM ** -0.5 * LOG2E
    cq, sq = cq_ref[...], sq_ref[...]
    cp, sp, cn, sn = cp_ref[...], sp_ref[...], cn_ref[...], sn_ref[...]
    n_ctx = kx_ref.shape[0]
    tq = q_ref.shape[0]
    win = tq + 2 * BLOCK
    qi = lax.broadcasted_iota(jnp.int32, (tq, win + n_ctx), 0)
    kj = lax.broadcasted_iota(jnp.int32, (tq, win + n_ctx), 1)
    rel = kj - BLOCK - qi
    off_seq = ((kj < BLOCK) & (nb == 0)) | ((kj >= tq + BLOCK) & (nb == n_steps - 1))
    mask = jnp.logical_not((kj < win) & ((rel > BLOCK) | (rel < -BLOCK) | off_seq))
    for kv in range(A_KV_HEADS):
        ks = slice(kv * HEAD_DIM, (kv + 1) * HEAD_DIM)
        k_all = jnp.concatenate([
            _rope_a(kp_ref[:, ks], cp, sp).astype(BF16),
            _rope_a(kc_ref[:, ks], cq, sq).astype(BF16),
            _rope_a(kn_ref[:, ks], cn, sn).astype(BF16),
            kx_ref[:, ks].astype(BF16)], axis=0)
        v_all = jnp.concatenate([r[:, ks].astype(BF16) for r in (vp_ref, vc_ref, vn_ref, vx_ref)], axis=0)
        for g in range(A_GROUP):
            head = kv * A_GROUP + g
            sl = slice(head * HEAD_DIM, (head + 1) * HEAD_DIM)
            q = (_rope_a(q_ref[:, sl], cq, sq) * scale).astype(BF16)
            s = jnp.where(mask, _dot_nt(q, k_all), NEG)
            o = _sink_softmax_pv(s, sink_ref[layer, head] * LOG2E, v_all)
            o_ref[:, sl] = o.astype(o_ref.dtype)


def _attn_a_calls(p, sink, ck, cv, cos_a, sin_a, layer, dims):
    t = p.shape[0]
    batch, seq, dec_batch, dec_seq = dims
    n_ctx = batch * seq
    smem = pl.BlockSpec(memory_space=pltpu.SMEM)
    aw = A_HEADS * HEAD_DIM
    kw = A_KV_HEADS * HEAD_DIM
    qcol, kcol, vcol = COL_QA * LANES // aw, COL_KA * LANES // kw, COL_VA * LANES // kw
    oa_ctx = pl.pallas_call(
        functools.partial(_attn_a_ctx_kernel, layer=layer),
        grid=(batch,),
        in_specs=[
            smem,
            pl.BlockSpec((seq, aw), lambda b: (b, qcol)),
            pl.BlockSpec((seq, kw), lambda b: (b, kcol)),
            pl.BlockSpec((seq, kw), lambda b: (b, vcol)),
            pl.BlockSpec(memory_space=pl.ANY),
        ],
        out_specs=pl.BlockSpec((seq, aw), lambda b: (b, 0)),
        out_shape=jax.ShapeDtypeStruct((t, aw), BF16),
        input_output_aliases={4: 0},
        compiler_params=_cparams("parallel"),
        name="attn_a_ctx",
    )(sink, p, p, p, jnp.zeros((t, aw), BF16))

    nbl = dec_seq // BLOCK
    base = n_ctx // BLOCK
    qb = 2 if (nbl % 2 == 0 and n_ctx % (2 * BLOCK) == 0) else 1
    tq = qb * BLOCK
    steps = nbl // qb

    def run(b, n):
        return n_ctx // tq + b * steps + n

    def prev(n):
        return jnp.maximum(qb * n - 1, 0)

    def nxt(n):
        return jnp.minimum(qb * n + qb, nbl - 1)

    def blk(b, i):
        return base + b * nbl + i

    past = ck.shape[2]
    kblk = (BLOCK, kw)
    tblk = (BLOCK, HEAD_DIM)
    oa_lat = pl.pallas_call(
        functools.partial(_attn_a_lat_kernel, layer=layer, n_steps=steps),
        grid=(dec_batch, steps),
        in_specs=[
            smem,
            pl.BlockSpec((tq, aw), lambda b, n: (run(b, n), qcol)),
            pl.BlockSpec(kblk, lambda b, n: (blk(b, prev(n)), kcol)),
            pl.BlockSpec((tq, kw), lambda b, n: (run(b, n), kcol)),
            pl.BlockSpec(kblk, lambda b, n: (blk(b, nxt(n)), kcol)),
            pl.BlockSpec(kblk, lambda b, n: (blk(b, prev(n)), vcol)),
            pl.BlockSpec((tq, kw), lambda b, n: (run(b, n), vcol)),
            pl.BlockSpec(kblk, lambda b, n: (blk(b, nxt(n)), vcol)),
            pl.BlockSpec((None, None, past, kw), lambda b, n: (b, layer, 0, 0)),
            pl.BlockSpec((None, None, past, kw), lambda b, n: (b, layer, 0, 0)),
            pl.BlockSpec((tq, HEAD_DIM), lambda b, n: (n, 0)),
            pl.BlockSpec((tq, HEAD_DIM), lambda b, n: (n, 0)),
            pl.BlockSpec(tblk, lambda b, n: (prev(n), 0)),
            pl.BlockSpec(tblk, lambda b, n: (prev(n), 0)),
            pl.BlockSpec(tblk, lambda b, n: (nxt(n), 0)),
            pl.BlockSpec(tblk, lambda b, n: (nxt(n), 0)),
            pl.BlockSpec(memory_space=pl.ANY),
        ],
        out_specs=pl.BlockSpec((tq, aw), lambda b, n: (run(b, n), 0)),
        out_shape=jax.ShapeDtypeStruct((t, aw), BF16),
        input_output_aliases={16: 0},
        compiler_params=_cparams("parallel", "parallel"),
        name="attn_a_lat",
    )(sink, p, p, p, p, p, p, p, ck, cv, cos_a, sin_a, cos_a, sin_a, cos_a, sin_a, oa_ctx)
    return oa_lat


def _diff_lambda(lv, lam_init):
    a = jnp.sum(lv[0:1, :] * lv[1:2, :], axis=-1, keepdims=True)
    b = jnp.sum(lv[2:3, :] * lv[3:4, :], axis=-1, keepdims=True)
    return jnp.exp(a) - jnp.exp(b) + lam_init


def _diff_core(q, k, v, lam, dn_w, lam_init):
    lane = lax.broadcasted_iota(jnp.int32, q.shape, 1)
    qs = q.astype(F32) * (B_HALF ** -0.5 * LOG2E)
    outs = []
    for half in range(2):
        sel = (lane < B_HALF) if half == 0 else (lane >= B_HALF)
        s = _dot_nt(jnp.where(sel, qs, 0.0).astype(BF16), k)
        e = jnp.exp2(s - jnp.max(s, axis=-1, keepdims=True))
        den = jnp.sum(e, axis=-1, keepdims=True)
        outs.append(_dot(e.astype(BF16), v) / den)
    o = outs[0] - lam * outs[1]
    ms = jnp.mean(o * o, axis=-1, keepdims=True)
    return o * lax.rsqrt(ms + EPS) * dn_w * (1.0 - lam_init)


def _attn_b_ctx_kernel(lv_ref, dn_ref, q_ref, k_ref, v_ref, buf_ref, o_ref, *, lam_init):
    del buf_ref
    lam = _diff_lambda(lv_ref[...], lam_init)
    for head in range(B_HEADS):
        sl = slice(head * HEAD_DIM, (head + 1) * HEAD_DIM)
        o = _diff_core(q_ref[:, sl], k_ref[:, sl].astype(BF16), v_ref[:, sl].astype(BF16), lam, dn_ref[...],
                       lam_init)
        o_ref[:, sl] = o.astype(o_ref.dtype)


def _attn_b_lat_kernel(lv_ref, dn_ref, q_ref, k_ref, v_ref, kx_ref, vx_ref, cq_ref, sq_ref,
                       ck_ref, sk_ref, ctx_rows_ref, o_ref, k_sc, v_sc, *, lam_init):
    del ctx_rows_ref
    n = k_ref.shape[0]

    @pl.when(pl.program_id(2) == 0)
    def _():
        k_sc[0:n, :] = _rope_d(k_ref[...], ck_ref[...], sk_ref[...]).astype(BF16)
        k_sc[n:, :] = kx_ref[...].astype(BF16)
        v_sc[0:n, :] = v_ref[...].astype(BF16)
        v_sc[n:, :] = vx_ref[...].astype(BF16)

    lam = _diff_lambda(lv_ref[...], lam_init)
    q = _rope_d(q_ref[...], cq_ref[...], sq_ref[...])
    o = _diff_core(q, k_sc[...], v_sc[...], lam, dn_ref[...], lam_init)
    o_ref[...] = o.astype(o_ref.dtype)


def _attn_b_calls(p, lv, dn_w, ck, cv, cos_d, sin_d, layer, lam_init, dims, tq):
    batch, seq, dec_batch, dec_seq = dims
    n_ctx = batch * seq
    hd = HEAD_DIM
    bw = B_HEADS * hd
    od_ctx = pl.pallas_call(
        functools.partial(_attn_b_ctx_kernel, lam_init=lam_init),
        grid=(batch,),
        in_specs=[
            pl.BlockSpec((4, B_HALF), lambda b: (0, 0)),
            pl.BlockSpec((1, hd), lambda b: (0, 0)),
            pl.BlockSpec((seq, bw), lambda b: (b, COL_QD * LANES // bw)),
            pl.BlockSpec((seq, bw), lambda b: (b, COL_KD * LANES // bw)),
            pl.BlockSpec((seq, bw), lambda b: (b, COL_VD * LANES // bw)),
            pl.BlockSpec(memory_space=pl.ANY),
        ],
        out_specs=pl.BlockSpec((seq, bw), lambda b: (b, 0)),
        out_shape=jax.ShapeDtypeStruct((p.shape[0], bw), BF16),
        input_output_aliases={5: 0},
        compiler_params=_cparams("parallel"),
        name="attn_b_ctx",
    )(lv, dn_w, p, p, p, jnp.zeros((p.shape[0], bw), BF16))

    past = ck.shape[2]
    nq = dec_seq // tq
    qbase = n_ctx // tq
    sbase = n_ctx // dec_seq
    od_lat = pl.pallas_call(
        functools.partial(_attn_b_lat_kernel, lam_init=lam_init),
        grid=(dec_batch, B_HEADS, nq),
        in_specs=[
            pl.BlockSpec((4, B_HALF), lambda b, h, i: (0, 0)),
            pl.BlockSpec((1, hd), lambda b, h, i: (0, 0)),
            pl.BlockSpec((tq, hd), lambda b, h, i: (qbase + b * nq + i, COL_QD + h)),
            pl.BlockSpec((dec_seq, hd), lambda b, h, i: (sbase + b, COL_KD + h)),
            pl.BlockSpec((dec_seq, hd), lambda b, h, i: (sbase + b, COL_VD + h)),
            pl.BlockSpec((None, None, past, hd), lambda b, h, i: (b, layer, 0, h)),
            pl.BlockSpec((None, None, past, hd), lambda b, h, i: (b, layer, 0, h)),
            pl.BlockSpec((tq, hd), lambda b, h, i: (i, 0)),
            pl.BlockSpec((tq, hd), lambda b, h, i: (i, 0)),
            pl.BlockSpec((dec_seq, hd), lambda b, h, i: (0, 0)),
            pl.BlockSpec((dec_seq, hd), lambda b, h, i: (0, 0)),
            pl.BlockSpec(memory_space=pl.ANY),
        ],
        out_specs=pl.BlockSpec((tq, hd), lambda b, h, i: (qbase + b * nq + i, h)),
        out_shape=jax.ShapeDtypeStruct((p.shape[0], B_HEADS * hd), BF16),
        input_output_aliases={11: 0},
        scratch_shapes=[pltpu.VMEM((dec_seq + past, hd), BF16), pltpu.VMEM((dec_seq + past, hd), BF16)],
        compiler_params=_cparams("parallel", "parallel", "arbitrary"),
        name="attn_b_lat",
    )(lv, dn_w, p, p, p, ck, cv, cos_d, sin_d, cos_d, sin_d, od_ctx)
    return od_lat


def _expand_heads(a, e3):
    return _dot(jnp.concatenate(_split3(a), axis=1), e3)


def _ssd_direction(xact, dt, a_row, e3, h_sc, hoff, fwd, dskip):
    L = C_CHUNK
    li = lax.broadcasted_iota(jnp.int32, (L, L), 0)
    si = lax.broadcasted_iota(jnp.int32, (L, L), 1)
    tri = (li >= si) if fwd else (li <= si)
    tri_b = jnp.where(tri, 1.0, 0.0).astype(BF16)
    dta = dt * a_row
    cum = _dot(jnp.concatenate([tri_b, tri_b, tri_b], axis=1), jnp.concatenate(_split3(dta), axis=0))
    cum_t = cum.T
    dt_t = dt.T
    end = cum[L - 1:L, :] if fwd else cum[0:1, :]
    to_end = jnp.exp(end - cum) * dt
    ecum = jnp.exp(cum)
    x = xact[:, :C_INNER]
    x_te = (x * _expand_heads(to_end, e3)).astype(BF16)
    dec_row = _expand_heads(jnp.broadcast_to(jnp.exp(end), (8, LANES)), e3)[0:1, :]
    lane = lax.broadcasted_iota(jnp.int32, (L, LANES), 1)
    lo = lane < C_HEAD_DIM
    gw = C_INNER // C_GROUPS
    ys, offs = [], []
    for g in range(C_GROUPS):
        bm = xact[:, C_INNER + g * C_STATE:C_INNER + (g + 1) * C_STATE]
        cm = xact[:, C_INNER + (C_GROUPS + g) * C_STATE:C_INNER + (C_GROUPS + g + 1) * C_STATE]
        cmb = cm.astype(BF16)
        cb = _dot_nt(cmb, bm.astype(BF16))
        h_prev = h_sc[:, g * gw:(g + 1) * gw]
        offs.append(_dot(cmb, h_prev.astype(BF16)))
        for pair in range(gw // LANES):
            blk = g * (gw // LANES) + pair
            w_parts = []
            for j in range(2):
                c = hoff + 2 * blk + j
                seg = cum[:, c:c + 1] - cum_t[c:c + 1, :]
                w = cb * jnp.exp(jnp.where(tri, seg, NEG)) * dt_t[c:c + 1, :]
                w_parts.append(w.astype(BF16))
            xb = x[:, blk * LANES:(blk + 1) * LANES]
            rhs = jnp.concatenate([jnp.where(lo, xb, 0.0), jnp.where(lo, 0.0, xb)], axis=0).astype(BF16)
            ys.append(_dot(jnp.concatenate(w_parts, axis=1), rhs))
        st = _dot(bm.T.astype(BF16), x_te[:, g * gw:(g + 1) * gw])
        h_sc[:, g * gw:(g + 1) * gw] = dec_row[:, g * gw:(g + 1) * gw] * h_prev + st
    y = jnp.concatenate(ys, axis=1) + jnp.concatenate(offs, axis=1) * _expand_heads(ecum, e3)
    if dskip is not None:
        y = y + dskip * x
    return y


def _conv_kernel(*refs, ncol, n_ctx_tiles, ctx_tps, lat_tps):
    x_refs, xp_refs, xn_refs = refs[:ncol], refs[ncol:2 * ncol], refs[2 * ncol:3 * ncol]
    cw_ref, cbias_ref, o_ref = refs[3 * ncol:]
    t = pl.program_id(0)
    is_ctx = t < n_ctx_tiles
    pos = jnp.where(is_ctx, t % ctx_tps, (t - n_ctx_tiles) % lat_tps)
    tps = jnp.where(is_ctx, ctx_tps, lat_tps)
    rows = o_ref.shape[0]

    def cols(rs):
        return jnp.concatenate([r[...].astype(F32) for r in rs], axis=1)

    xp = jnp.concatenate([jnp.where(pos > 0, cols(xp_refs), 0.0), cols(x_refs),
                          jnp.where(pos < tps - 1, cols(xn_refs), 0.0)], axis=0)
    n = rows + 2 * HALO
    acc = jnp.broadcast_to(cbias_ref[...], o_ref.shape)
    for k in range(C_CONV):
        shifted = xp if k == C_CONV // 2 else pltpu.roll(xp, (C_CONV // 2 - k) % n, axis=0)
        acc = acc + shifted[HALO:HALO + rows] * cw_ref[k:k + 1, :]
    o_ref[...] = acc * _sigmoid(acc)


def _conv_call(p, conv_w, conv_b, layer, dims, rows=512):
    batch, seq, dec_batch, dec_seq = dims
    t = p.shape[0]
    rows = min(rows, seq)
    assert seq % rows == 0 and dec_seq % rows == 0
    hpt = rows // HALO
    last_halo = t // HALO - 1
    cw = HALF_INNER
    c0 = COL_XBC * LANES // cw
    ncol = C_CONV_CH // cw
    main = [pl.BlockSpec((rows, cw), lambda i, c=c: (i, c0 + c)) for c in range(ncol)]
    prev = [pl.BlockSpec((HALO, cw), lambda i, c=c: (jnp.maximum(i * hpt - 1, 0), c0 + c)) for c in range(ncol)]
    nxt = [pl.BlockSpec((HALO, cw), lambda i, c=c: (jnp.minimum((i + 1) * hpt, last_halo), c0 + c))
           for c in range(ncol)]
    return pl.pallas_call(
        functools.partial(_conv_kernel, ncol=ncol, n_ctx_tiles=batch * seq // rows, ctx_tps=seq // rows,
                          lat_tps=dec_seq // rows),
        grid=(t // rows,),
        in_specs=main + prev + nxt + [
            pl.BlockSpec((None, C_CONV, C_CONV_CH), lambda i: (layer, 0, 0)),
            pl.BlockSpec((None, 1, C_CONV_CH), lambda i: (layer, 0, 0)),
        ],
        out_specs=pl.BlockSpec((rows, C_CONV_CH), lambda i: (i, 0)),
        out_shape=jax.ShapeDtypeStruct((t, C_CONV_CH), F32),
        compiler_params=_cparams("parallel"),
        name="conv",
    )(*([p] * (3 * ncol)), conv_w, conv_b)


def _ssd_kernel(xf_ref, xb_ref, dtf_ref, dtb_ref,
                dtbias_ref, alog_ref, dskip_ref, e3_ref, h0f_ref, h0b_ref,
                yf_ref, yb_ref, hfo_ref, hbo_ref, hf_sc, hb_sc,
                *, n_ctx_chunks, ctx_cps, lat_cps):
    s = pl.program_id(0)
    is_ctx = s < n_ctx_chunks
    pos = jnp.where(is_ctx, s % ctx_cps, (s - n_ctx_chunks) % lat_cps)
    cps = jnp.where(is_ctx, ctx_cps, lat_cps)
    first = pos == 0
    last = pos == cps - 1

    @pl.when(first & is_ctx)
    def _():
        hf_sc[...] = jnp.zeros_like(hf_sc)
        hb_sc[...] = jnp.zeros_like(hb_sc)

    @pl.when(first & jnp.logical_not(is_ctx))
    def _():
        hf_sc[...] = h0f_ref[...].T
        hb_sc[...] = h0b_ref[...].T

    def softplus(v):
        return jnp.maximum(v, 0.0) + jnp.log(1.0 + jnp.exp(-jnp.abs(v)))

    a_row = -jnp.exp(alog_ref[...])
    e3 = e3_ref[...]

    group = xf_ref.shape[0] // C_CHUNK
    for c in range(group):
        rows = slice(c * C_CHUNK, (c + 1) * C_CHUNK)
        dt = softplus(dtf_ref[rows, :] + dtbias_ref[...])
        yf = _ssd_direction(xf_ref[rows, :], dt, a_row, e3[0], hf_sc, 0, True, dskip_ref[...])
        yf_ref[rows, :] = yf.astype(yf_ref.dtype)
    for c in reversed(range(group)):
        rows = slice(c * C_CHUNK, (c + 1) * C_CHUNK)
        dt = softplus(dtb_ref[rows, :] + dtbias_ref[...])
        yb = _ssd_direction(xb_ref[rows, :], dt, a_row, e3[1], hb_sc, C_HEADS, False, None)
        yb_ref[rows, :] = yb.astype(yb_ref.dtype)

    @pl.when(last & is_ctx)
    def _():
        hfo_ref[...] = hf_sc[...].T.reshape(hfo_ref.shape)
        hbo_ref[...] = hb_sc[...].T.reshape(hbo_ref.shape)


def _ssd_call(xact, dt, dtbias, alog, dskip_e, e3, h0f, h0b, layer, dims):
    batch, seq, dec_batch, dec_seq = dims
    group = 2 if (seq % (2 * C_CHUNK) == 0 and dec_seq % (2 * C_CHUNK) == 0) else 1
    L = group * C_CHUNK
    ctx_cps, lat_cps = seq // L, dec_seq // L
    n_ctx_chunks = batch * ctx_cps
    n_chunks = n_ctx_chunks + dec_batch * lat_cps
    t = xact.shape[0]

    def mirror(s):
        c_ctx = (s // ctx_cps) * ctx_cps + (ctx_cps - 1 - s % ctx_cps)
        r = s - n_ctx_chunks
        c_lat = n_ctx_chunks + (r // lat_cps) * lat_cps + (lat_cps - 1 - r % lat_cps)
        return jnp.where(s < n_ctx_chunks, c_ctx, c_lat)

    def lat_b(s):
        return jnp.maximum(s - n_ctx_chunks, 0) // lat_cps

    def ctx_b(s):
        return jnp.minimum(s // ctx_cps, batch - 1)

    const2 = lambda s: (0, 0)
    hspec = pl.BlockSpec((None, None, C_INNER, C_STATE), lambda s: (lat_b(s), layer, 0, 0))
    ospec = pl.BlockSpec((None, C_HEADS, C_HEAD_DIM, C_STATE), lambda s: (ctx_b(s), 0, 0, 0))
    return pl.pallas_call(
        functools.partial(_ssd_kernel, n_ctx_chunks=n_ctx_chunks, ctx_cps=ctx_cps, lat_cps=lat_cps),
        grid=(n_chunks,),
        in_specs=[
            pl.BlockSpec((L, C_CONV_CH), lambda s: (s, 0)),
            pl.BlockSpec((L, C_CONV_CH), lambda s: (mirror(s), 0)),
            pl.BlockSpec((L, LANES), lambda s: (s, 0)),
            pl.BlockSpec((L, LANES), lambda s: (mirror(s), 0)),
            pl.BlockSpec((1, LANES), const2),
            pl.BlockSpec((1, LANES), const2),
            pl.BlockSpec((1, C_INNER), const2),
            pl.BlockSpec((2, 3 * LANES, C_INNER), lambda s: (0, 0, 0)),
            hspec, hspec,
        ],
        out_specs=[
            pl.BlockSpec((L, C_INNER), lambda s: (s, 0)),
            pl.BlockSpec((L, C_INNER), lambda s: (mirror(s), 0)),
            ospec, ospec,
        ],
        out_shape=[
            jax.ShapeDtypeStruct((t, C_INNER), BF16), jax.ShapeDtypeStruct((t, C_INNER), BF16),
            jax.ShapeDtypeStruct((batch, C_HEADS, C_HEAD_DIM, C_STATE), F32),
            jax.ShapeDtypeStruct((batch, C_HEADS, C_HEAD_DIM, C_STATE), F32),
        ],
        scratch_shapes=[
            pltpu.VMEM((C_STATE, C_INNER), F32),
            pltpu.VMEM((C_STATE, C_INNER), F32),
        ],
        compiler_params=_cparams("arbitrary"),
        name="ssd",
    )(xact, xact, dt, dt, dtbias, alog, dskip_e, e3, h0f, h0b)


def _outproj_kernel(oa_ref, od_ref, yf_ref, yb_ref, z0_ref, z1_ref, sn_ref, h_ref, mod_ref, w_ref, o_ref):
    na = oa_ref.shape[1]
    nd = od_ref.shape[1]
    acc = _dot(oa_ref[...], w_ref[0:na, :]) + _dot(od_ref[...], w_ref[na:na + nd, :])
    gw = HALF_INNER
    for g, z_ref in enumerate((z0_ref, z1_ref)):
        z = z_ref[...].astype(F32)
        y = yf_ref[:, g * gw:(g + 1) * gw].astype(F32) + yb_ref[:, g * gw:(g + 1) * gw].astype(F32)
        yg = y * (z * _sigmoid(z))
        yg = yg * lax.rsqrt(jnp.mean(yg * yg, axis=-1, keepdims=True) + EPS)
        yg = yg * sn_ref[:, g * gw:(g + 1) * gw]
        lo = na + nd + g * gw
        acc = acc + _dot(yg.astype(BF16), w_ref[lo:lo + gw, :])
    o_ref[...] = h_ref[...] + mod_ref[2:3, :] * acc


def _outproj_call(oa, od, yf, yb, p, ssm_norm, h, mod_l, w, layer, mod_row, tm):
    t, d = h.shape
    mw = w.shape[1]
    z0 = COL_Z * LANES // HALF_INNER
    return pl.pallas_call(
        _outproj_kernel,
        grid=(t // tm,),
        in_specs=[
            pl.BlockSpec((tm, oa.shape[1]), lambda i: (i, 0)),
            pl.BlockSpec((tm, od.shape[1]), lambda i: (i, 0)),
            pl.BlockSpec((tm, C_INNER), lambda i: (i, 0)),
            pl.BlockSpec((tm, C_INNER), lambda i: (i, 0)),
            pl.BlockSpec((tm, HALF_INNER), lambda i: (i, z0)),
            pl.BlockSpec((tm, HALF_INNER), lambda i: (i, z0 + 1)),
            pl.BlockSpec((1, C_INNER), lambda i: (0, 0)),
            pl.BlockSpec((tm, d), lambda i: (i, 0)),
            pl.BlockSpec((None, N_MOD, d), lambda i: (mod_row(i * tm), 0, 0)),
            pl.BlockSpec((None, mw, d), lambda i: (layer, 0, 0), pipeline_mode=pl.Buffered(1)),
        ],
        out_specs=pl.BlockSpec((tm, d), lambda i: (i, 0)),
        out_shape=jax.ShapeDtypeStruct((t, d), F32),
        compiler_params=_cparams("parallel"),
        name="outproj",
    )(oa, od, yf, yb, p, p, ssm_norm, h, mod_l, w)


def _ffn_kernel(h_ref, mod_ref, nw_ref, wg_ref, wu_ref, wd_ref, *rest, n_ctx_tiles=None):
    if n_ctx_tiles is None:
        o_ref, u_sc = rest
        _ffn_body(h_ref, mod_ref, nw_ref, wg_ref, wu_ref, wd_ref, o_ref, u_sc, None)
        return
    fw_ref, oc_ref, ol_ref, u_sc = rest
    i = pl.program_id(0)
    for cond, o_ref in ((i < n_ctx_tiles, oc_ref), (i >= n_ctx_tiles, ol_ref)):
        pl.when(cond)(functools.partial(_ffn_body, h_ref, mod_ref, nw_ref, wg_ref, wu_ref, wd_ref, o_ref, u_sc,
                                        fw_ref))


def _ffn_body(h_ref, mod_ref, nw_ref, wg_ref, wu_ref, wd_ref, o_ref, u_sc, fw_ref):
    f = pl.program_id(1)

    @pl.when(f == 0)
    def _():
        u_sc[...] = _modnorm(h_ref[...], nw_ref[...], mod_ref[3:4, :], mod_ref[4:5, :]).astype(BF16)
        o_ref[...] = jnp.zeros_like(o_ref)

    u = u_sc[...]
    half = wg_ref.shape[1] // 2
    part = None
    for lo in (0, half):
        g = _dot(u, wg_ref[:, lo:lo + half])
        a = (g * _sigmoid(g)) * _dot(u, wu_ref[:, lo:lo + half])
        d_half = _dot(a.astype(BF16), wd_ref[lo:lo + half, :])
        part = d_half if part is None else part + d_half
    o_ref[...] += part

    @pl.when(f == pl.num_programs(1) - 1)
    def _():
        y = h_ref[...] + mod_ref[5:6, :] * o_ref[...]
        if fw_ref is not None:
            y = y * lax.rsqrt(jnp.mean(y * y, axis=-1, keepdims=True) + EPS) * fw_ref[...]
        o_ref[...] = y


def _ffn_call(h, mod_l, nw, w_gu, w_d, layer, mod_row, tm, tf, final=None):
    t, d = h.shape
    ff = w_d.shape[1]
    nf = ff // tf
    in_specs = [
        pl.BlockSpec((tm, d), lambda i, f: (i, 0)),
        pl.BlockSpec((None, N_MOD, d), lambda i, f: (mod_row(i * tm), 0, 0)),
        pl.BlockSpec((1, d), lambda i, f: (0, 0)),
        pl.BlockSpec((None, d, tf), lambda i, f: (layer, 0, f)),
        pl.BlockSpec((None, d, tf), lambda i, f: (layer, 0, nf + f)),
        pl.BlockSpec((None, tf, d), lambda i, f: (layer, f, 0)),
    ]
    args = [h, mod_l, nw, w_gu, w_gu, w_d]
    if final is None:
        kern = _ffn_kernel
        out_specs = pl.BlockSpec((tm, d), lambda i, f: (i, 0))
        out_shape = jax.ShapeDtypeStruct((t, d), F32)
        sem = ("parallel", "arbitrary")
    else:
        fw, n_ctx = final
        nc = n_ctx // tm
        kern = functools.partial(_ffn_kernel, n_ctx_tiles=nc)
        in_specs.append(pl.BlockSpec((1, d), lambda i, f: (0, 0)))
        args.append(fw)
        out_specs = [pl.BlockSpec((tm, d), lambda i, f: (jnp.minimum(i, nc - 1), 0)),
                     pl.BlockSpec((tm, d), lambda i, f: (jnp.maximum(i - nc, 0), 0))]
        out_shape = [jax.ShapeDtypeStruct((n_ctx, d), F32), jax.ShapeDtypeStruct((t - n_ctx, d), F32)]
        sem = ("arbitrary", "arbitrary")
    return pl.pallas_call(
        kern,
        grid=(t // tm, nf),
        in_specs=in_specs,
        out_specs=out_specs,
        out_shape=out_shape,
        scratch_shapes=[pltpu.VMEM((tm, d), BF16)],
        compiler_params=_cparams(*sem),
        name="ffn",
    )(*args)


def _rope_tables(n, rot_dim):
    rows = n // GRID_W
    row = jnp.repeat(jnp.arange(rows), GRID_W).astype(F32)
    col = (jnp.arange(rows * GRID_W) % GRID_W).astype(F32)
    quarter = rot_dim // 4
    inv = ROPE_BASE ** (-jnp.arange(quarter, dtype=F32) / quarter)
    ang = jnp.concatenate([row[:, None] * inv, col[:, None] * inv], axis=-1)
    c, s = jnp.cos(ang), jnp.sin(ang)
    reps = LANES // rot_dim
    return (jnp.tile(jnp.concatenate([c, c], axis=-1), (1, reps)),
            jnp.tile(jnp.concatenate([-s, s], axis=-1), (1, reps)))


def _lambda_init(layer):
    return 0.8 - 0.6 * math.exp(-0.3 * layer)


def _pad_lanes(v):
    return jnp.pad(v.reshape(v.shape[0], 1, -1), ((0, 0), (0, 0), (0, LANES - DT_COLS)))


def kernel(x_prompt, x_sample, cache_attn_k, cache_attn_v, cache_diff_k, cache_diff_v, state_ssm_fwd, state_ssm_bwd, c, c_ctx, w_ada, b_ada, norm_mix, norm_ffn, w_in, attn_sink, diff_lambda, diff_norm, conv_w, conv_b, dt_bias, a_log, d_skip, ssm_norm, w_out, w_gate_up, w_down, norm_final):
    batch, seq, d = x_prompt.shape
    dec_batch, dec_seq, _ = x_sample.shape
    depth = w_in.shape[0]
    past = cache_attn_k.shape[2]
    n_ctx = batch * seq
    dims = (batch, seq, dec_batch, dec_seq)
    tm = 512 if (n_ctx % 512 == 0 and dec_seq % 512 == 0) else 256
    tm_big = 1024 if (n_ctx % 1024 == 0 and dec_seq % 1024 == 0) else tm
    tq = min(256, dec_seq)
    assert n_ctx % dec_seq == 0 and n_ctx % tm == 0 and dec_seq % tm == 0
    assert seq % C_CHUNK == 0 and dec_seq % C_CHUNK == 0 and dec_seq % GRID_W == 0
    assert 1 + dec_batch <= 8

    def mod_row(start):
        return jnp.where(start < n_ctx, 0, 1 + (start - n_ctx) // dec_seq)

    w_in_b = w_in.astype(BF16)
    w_in_dt = jnp.pad(w_in_b[:, :, P_COLS:], ((0, 0), (0, 0), (0, LANES - DT_COLS)))
    w_out_b = w_out.astype(BF16)
    w_gu_b = w_gate_up.astype(BF16)
    w_down_b = w_down.astype(BF16)
    cos_a, sin_a = _rope_tables(dec_seq, HEAD_DIM)
    cos_d, sin_d = _rope_tables(dec_seq, B_HALF)
    conv_b3 = conv_b.reshape(depth, 1, C_CONV_CH)
    dtbias = _pad_lanes(dt_bias)
    alog = _pad_lanes(a_log)
    dskip_e = jnp.repeat(d_skip, C_HEAD_DIM, axis=-1).reshape(depth, 1, C_INNER)
    head_of_lane = jnp.arange(C_INNER) // C_HEAD_DIM
    e_f = (jnp.arange(LANES)[:, None] == head_of_lane[None, :]).astype(BF16)
    e_b = (jnp.arange(LANES)[:, None] == head_of_lane[None, :] + C_HEADS).astype(BF16)
    e3 = jnp.stack([jnp.concatenate([e_f] * 3, axis=0), jnp.concatenate([e_b] * 3, axis=0)])
    ck_a = cache_attn_k.reshape(dec_batch, depth, past, A_KV_HEADS * HEAD_DIM)
    cv_a = cache_attn_v.reshape(dec_batch, depth, past, A_KV_HEADS * HEAD_DIM)
    ck_d = cache_diff_k.reshape(dec_batch, depth, past, B_HEADS * HEAD_DIM)
    cv_d = cache_diff_v.reshape(dec_batch, depth, past, B_HEADS * HEAD_DIM)
    h0f = state_ssm_fwd.reshape(dec_batch, depth, C_INNER, C_STATE)
    h0b = state_ssm_bwd.reshape(dec_batch, depth, C_INNER, C_STATE)

    cond8 = jnp.concatenate([c_ctx[None, :], c, jnp.zeros((8 - 1 - dec_batch, d), F32)], axis=0)
    mod = _ada_call(cond8, w_ada, b_ada)[:, :1 + dec_batch].reshape(depth, 1 + dec_batch, N_MOD, d)

    h = jnp.concatenate([x_prompt.reshape(n_ctx, d), x_sample.reshape(dec_batch * dec_seq, d)], axis=0)
    ctx_out = []
    for l in range(depth):
        lam_init = _lambda_init(l)
        p, dt, kv = _inproj_call(h, mod[l], norm_mix[l][None, :], w_in_b, w_in_dt, l, mod_row, n_ctx, tm_big)
        oa = _attn_a_calls(p, attn_sink, ck_a, cv_a, cos_a, sin_a, l, dims)
        od = _attn_b_calls(p, diff_lambda[l], diff_norm[l][None, :], ck_d, cv_d, cos_d, sin_d,
                           l, lam_init, dims, tq)
        xact = _conv_call(p, conv_w, conv_b3, l, dims)
        yf, yb, hf, hb = _ssd_call(xact, dt, dtbias[l], alog[l], dskip_e[l], e3, h0f, h0b, l, dims)
        h = _outproj_call(oa, od, yf, yb, p, ssm_norm[l][None, :], h, mod[l], w_out_b, l, mod_row, tm)
        final = (norm_final[None, :], n_ctx) if l == depth - 1 else None
        h = _ffn_call(h, mod[l], norm_ffn[l][None, :], w_gu_b, w_down_b, l, mod_row, tm, 512, final)

        def ctx_cols(lo, width, heads):
            return kv[:n_ctx, lo:lo + width].reshape(batch, seq, heads, HEAD_DIM)

        akv, bw = A_KV_HEADS * HEAD_DIM, B_HEADS * HEAD_DIM
        ctx_out.append((ctx_cols(0, akv, A_KV_HEADS), ctx_cols(akv, akv, A_KV_HEADS),
                        ctx_cols(2 * akv, bw, B_HEADS), ctx_cols(2 * akv + bw, bw, B_HEADS),
                        hf, hb))

    y_prompt, y_sample = h
    stacked = [jnp.stack([t[i] for t in ctx_out], axis=1) for i in range(6)]
    return (y_prompt.reshape(batch, seq, d), y_sample.reshape(dec_batch, dec_seq, d), *stacked)
```
